```python
import jax, jax.numpy as jnp
from jax import lax
import numpy as np

D_MODEL = 1024
BATCH = 16
SEQ = 4096
DEPTH = 4
DEC_BATCH = 2
DEC_SEQ = 16384
PAST_LEN = 128

D_MIX = D_MODEL
A_WIDTH = D_MIX // 2
A_GROUPS = 8
CONV_W = 3
B_WIDTH = D_MIX - A_WIDTH
B_HEADS = 4
B_HEAD_DIM = B_WIDTH // B_HEADS
CHUNK = 128
PROJ_OUT = 4 * A_WIDTH + 3 * B_WIDTH
ALPHA = (2.0 * DEPTH) ** 0.25
BETA = (8.0 * DEPTH) ** -0.25
LN_EPS = 1e-5

kernel_name = "hybrid_shortconv_gmlp_encoder"


def _layernorm(x, g, b):
    xf = x.astype(jnp.float32)
    mu = jnp.mean(xf, axis=-1, keepdims=True)
    var = jnp.mean(jnp.square(xf - mu), axis=-1, keepdims=True)
    y = (xf - mu) * lax.rsqrt(var + LN_EPS)
    return (y * g.astype(jnp.float32) + b.astype(jnp.float32)).astype(x.dtype)


def _layer(x, w_in, conv_w, conv_b, sg_w, sg_b, sg_norm_g, sg_norm_b, w_out, ln_g, ln_b):
    bsz, seq, _ = x.shape
    h = jnp.einsum('bsd,de->bse', x, w_in)
    a_in, a_b, a_c, a_z, b_u, b_v, b_z = jnp.split(
        h, [A_WIDTH, 2 * A_WIDTH, 3 * A_WIDTH, 4 * A_WIDTH,
            4 * A_WIDTH + B_WIDTH, 4 * A_WIDTH + 2 * B_WIDTH], axis=-1)

    t = a_c * a_in
    tp = jnp.pad(t, ((0, 0), (1, 1), (0, 0)))
    conv = (conv_w[0] * tp[:, :-2] + conv_w[1] * tp[:, 1:-1]
            + conv_w[2] * tp[:, 2:] + conv_b)
    out_a = a_b * conv * jax.nn.silu(a_z)

    n_chunks = seq // CHUNK
    u = jax.nn.gelu(b_u)
    v = jax.nn.gelu(b_v).reshape(bsz, n_chunks, CHUNK, B_HEADS, B_HEAD_DIM)
    v = _layernorm(v, sg_norm_g.reshape(B_HEADS, B_HEAD_DIM), sg_norm_b.reshape(B_HEADS, B_HEAD_DIM))
    s = jnp.einsum('hpq,bnqhd->bnphd', sg_w, v) + jnp.transpose(sg_b)[None, None, :, :, None]
    s = s.reshape(bsz, seq, B_WIDTH)
    out_b = u * s * jax.nn.silu(b_z)

    y = jnp.einsum('bse,ed->bsd', jnp.concatenate([out_a, out_b], axis=-1), w_out)
    return _layernorm(ALPHA * x + y, ln_g, ln_b)


def _trunk(x, w_in, conv_w, conv_b, sg_w, sg_b, sg_norm_g, sg_norm_b, w_out, ln_g, ln_b):
    for l in range(DEPTH):
        x = _layer(x, w_in[l], conv_w[l], conv_b[l], sg_w[l], sg_b[l],
                   sg_norm_g[l], sg_norm_b[l], w_out[l], ln_g[l], ln_b[l])
    return x


def setup_inputs(seed: int = 0) -> dict:
    key = jax.random.key(seed)
    ks = jax.random.split(key, 12)
    f32 = jnp.float32
    x_prompt = jax.random.normal(ks[0], (BATCH, SEQ, D_MODEL), f32)
    x_sample = jax.random.normal(ks[1], (DEC_BATCH, DEC_SEQ, D_MODEL), f32)
    w_in = jax.random.normal(ks[2], (DEPTH, D_MODEL, PROJ_OUT), f32) * D_MODEL ** -0.5
    conv_w = jax.random.normal(ks[3], (DEPTH, CONV_W, A_WIDTH), f32) * CONV_W ** -0.5
    conv_b = jax.random.normal(ks[4], (DEPTH, A_WIDTH), f32) * 0.02
    sg_w = jax.random.normal(ks[5], (DEPTH, B_HEADS, CHUNK, CHUNK), f32) * CHUNK ** -0.5
    sg_b = 1.0 + 0.1 * jax.random.normal(ks[6], (DEPTH, B_HEADS, CHUNK), f32)
    sg_norm_g = 1.0 + 0.02 * jax.random.normal(ks[7], (DEPTH, B_WIDTH), f32)
    sg_norm_b = 0.02 * jax.random.normal(ks[8], (DEPTH, B_WIDTH), f32)
    w_out = jax.random.normal(ks[9], (DEPTH, D_MIX, D_MODEL), f32) * (D_MIX ** -0.5) * BETA
    ln_g = 1.0 + 0.02 * jax.random.normal(ks[10], (DEPTH, D_MODEL), f32)
    ln_b = 0.02 * jax.random.normal(ks[11], (DEPTH, D_MODEL), f32)
    return {"x_prompt": x_prompt, "x_sample": x_sample, "w_in": w_in, "conv_w": conv_w,
            "conv_b": conv_b, "sg_w": sg_w, "sg_b": sg_b, "sg_norm_g": sg_norm_g,
            "sg_norm_b": sg_norm_b, "w_out": w_out, "ln_g": ln_g, "ln_b": ln_b}


def reference(x_prompt, x_sample, w_in, conv_w, conv_b, sg_w, sg_b, sg_norm_g, sg_norm_b,
              w_out, ln_g, ln_b):
    y_prompt = _trunk(x_prompt, w_in, conv_w, conv_b, sg_w, sg_b, sg_norm_g, sg_norm_b,
                      w_out, ln_g, ln_b)
    y_sample = _trunk(x_sample, w_in, conv_w, conv_b, sg_w, sg_b, sg_norm_g, sg_norm_b,
                      w_out, ln_g, ln_b)
    return (y_prompt, y_sample)
```

```python
import functools
import math

import jax
import jax.numpy as jnp
from jax import lax
from jax.experimental import pallas as pl
from jax.experimental.pallas import tpu as pltpu

D_MODEL = 1024
DEPTH = 4
A_WIDTH = 512
B_WIDTH = 512
B_HEADS = 4
B_HEAD_DIM = 128
CHUNK = 128
PROJ_OUT = 4 * A_WIDTH + 3 * B_WIDTH
ALPHA = (2.0 * DEPTH) ** 0.25
LN_EPS = 1e-5
GELU_C0 = math.sqrt(2.0 / math.pi)
GELU_C1 = 0.044715

COL_A_IN = 0
COL_A_B = A_WIDTH
COL_A_C = 2 * A_WIDTH
COL_A_Z = 3 * A_WIDTH
COL_B_U = 4 * A_WIDTH
COL_B_V = 4 * A_WIDTH + B_WIDTH
COL_B_Z = 4 * A_WIDTH + 2 * B_WIDTH

HALO = 16
SEQ_TILE = 512
ROW_BLOCK = 256
VMEM_LIMIT_BYTES = 56 * 1024 * 1024


def _gelu_tanh(x):
    return 0.5 * x * (1.0 + jnp.tanh(GELU_C0 * (x + GELU_C1 * (x * x * x))))


def _silu(x):
    return x * (1.0 / (1.0 + jnp.exp(-x)))


def _dot(a, b):
    return jnp.dot(a, b, preferred_element_type=jnp.float32)


def _layer_kernel(x_ref, xprev_ref, xnext_ref, w_in_ref, conv_w_ref, conv_b_ref,
                  sg_w_ref, sg_bias_ref, sg_g_ref, sg_b_ref, w_out_ref,
                  ln_g_ref, ln_b_ref, o_ref, xb_ref):
    i = pl.program_id(1)
    n_i = pl.num_programs(1)

    prev = xprev_ref[0].astype(jnp.bfloat16)
    nxt = xnext_ref[0].astype(jnp.bfloat16)
    xb_ref[0:HALO, :] = jnp.where(i > 0, prev, jnp.zeros_like(prev))
    xb_ref[HALO:HALO + SEQ_TILE, :] = x_ref[0].astype(jnp.bfloat16)
    xb_ref[HALO + SEQ_TILE:, :] = jnp.where(i < n_i - 1, nxt, jnp.zeros_like(nxt))

    ext = ROW_BLOCK + 2 * HALO

    def row_block(j, carry):
        r0 = pl.multiple_of(j * ROW_BLOCK, ROW_BLOCK)
        xe = xb_ref[pl.ds(r0, ext), :]
        xm = xb_ref[pl.ds(r0 + HALO, ROW_BLOCK), :]

        a_in = _dot(xe, w_in_ref[:, COL_A_IN:COL_A_IN + A_WIDTH])
        a_c = _dot(xe, w_in_ref[:, COL_A_C:COL_A_C + A_WIDTH])
        t = a_c * a_in
        t_m1 = pltpu.roll(t, 1, axis=0)[HALO:HALO + ROW_BLOCK]
        t_p1 = pltpu.roll(t, ext - 1, axis=0)[HALO:HALO + ROW_BLOCK]
        t_0 = t[HALO:HALO + ROW_BLOCK]
        conv = (conv_w_ref[0:1, :] * t_m1 + conv_w_ref[1:2, :] * t_0
                + conv_w_ref[2:3, :] * t_p1 + conv_b_ref[...])
        a_b = _dot(xm, w_in_ref[:, COL_A_B:COL_A_B + A_WIDTH])
        a_z = _dot(xm, w_in_ref[:, COL_A_Z:COL_A_Z + A_WIDTH])
        out_a = a_b * conv * _silu(a_z)

        b_v = _dot(xm, w_in_ref[:, COL_B_V:COL_B_V + B_WIDTH])
        v = _gelu_tanh(b_v)
        n_chunks = ROW_BLOCK // CHUNK
        s_heads = []
        for h in range(B_HEADS):
            vh = v[:, h * B_HEAD_DIM:(h + 1) * B_HEAD_DIM]
            mu = jnp.mean(vh, axis=-1, keepdims=True)
            d = vh - mu
            var = jnp.mean(d * d, axis=-1, keepdims=True)
            vn = (d * lax.rsqrt(var + LN_EPS)
                  * sg_g_ref[:, h * B_HEAD_DIM:(h + 1) * B_HEAD_DIM]
                  + sg_b_ref[:, h * B_HEAD_DIM:(h + 1) * B_HEAD_DIM])
            vn = vn.astype(jnp.bfloat16)
            rhs = jnp.concatenate(
                [vn[c * CHUNK:(c + 1) * CHUNK] for c in range(n_chunks)], axis=1)
            s2 = _dot(sg_w_ref[h], rhs)
            s_heads.append(jnp.concatenate(
                [s2[:, c * B_HEAD_DIM:(c + 1) * B_HEAD_DIM] for c in range(n_chunks)],
                axis=0))
        s = jnp.concatenate(s_heads, axis=1)
        bias = sg_bias_ref[...]
        s = s + jnp.concatenate([bias] * n_chunks, axis=0)
        b_u = _dot(xm, w_in_ref[:, COL_B_U:COL_B_U + B_WIDTH])
        b_z = _dot(xm, w_in_ref[:, COL_B_Z:COL_B_Z + B_WIDTH])
        out_b = _gelu_tanh(b_u) * s * _silu(b_z)

        mixed = jnp.concatenate([out_a, out_b], axis=1).astype(jnp.bfloat16)
        y = _dot(mixed, w_out_ref[...])
        r = ALPHA * x_ref[0, pl.ds(r0, ROW_BLOCK), :] + y
        mu = jnp.mean(r, axis=-1, keepdims=True)
        d = r - mu
        var = jnp.mean(d * d, axis=-1, keepdims=True)
        o_ref[0, pl.ds(r0, ROW_BLOCK), :] = (
            d * lax.rsqrt(var + LN_EPS) * ln_g_ref[...] + ln_b_ref[...])
        return carry

    lax.fori_loop(0, SEQ_TILE // ROW_BLOCK, row_block, 0)


def _layer(x, w_in, conv_w, conv_b, sg_w, sg_bias, sg_g, sg_b, w_out, ln_g, ln_b):
    bsz, seq, _ = x.shape
    assert seq % SEQ_TILE == 0 and SEQ_TILE % ROW_BLOCK == 0 and ROW_BLOCK % CHUNK == 0
    n_tiles = seq // SEQ_TILE
    halo_per_tile = SEQ_TILE // HALO
    n_halo_blocks = seq // HALO

    def const(shape):
        return pl.BlockSpec(shape, lambda b, i: (0,) * len(shape))

    in_specs = [
        pl.BlockSpec((1, SEQ_TILE, D_MODEL), lambda b, i: (b, i, 0)),
        pl.BlockSpec((1, HALO, D_MODEL),
                     lambda b, i: (b, jnp.maximum(i * halo_per_tile - 1, 0), 0)),
        pl.BlockSpec((1, HALO, D_MODEL),
                     lambda b, i: (b, jnp.minimum((i + 1) * halo_per_tile,
                                                  n_halo_blocks - 1), 0)),
        const((D_MODEL, PROJ_OUT)),
        const((3, A_WIDTH)),
        const((1, A_WIDTH)),
        const((B_HEADS, CHUNK, CHUNK)),
        const((CHUNK, B_WIDTH)),
        const((1, B_WIDTH)),
        const((1, B_WIDTH)),
        const((D_MODEL, D_MODEL)),
        const((1, D_MODEL)),
        const((1, D_MODEL)),
    ]
    return pl.pallas_call(
        _layer_kernel,
        grid=(bsz, n_tiles),
        in_specs=in_specs,
        out_specs=pl.BlockSpec((1, SEQ_TILE, D_MODEL), lambda b, i: (b, i, 0)),
        out_shape=jax.ShapeDtypeStruct(x.shape, jnp.float32),
        scratch_shapes=[pltpu.VMEM((SEQ_TILE + 2 * HALO, D_MODEL), jnp.bfloat16)],
        compiler_params=pltpu.CompilerParams(
            dimension_semantics=("arbitrary", "arbitrary"),
            vmem_limit_bytes=VMEM_LIMIT_BYTES),
        name="encoder_layer",
    )(x, x, x, w_in, conv_w, conv_b, sg_w, sg_bias, sg_g, sg_b, w_out, ln_g, ln_b)


def _trunk(x, params):
    for l in range(DEPTH):
        x = _layer(x, *(p[l] for p in params))
    return x


def kernel(x_prompt, x_sample, w_in, conv_w, conv_b, sg_w, sg_b, sg_norm_g, sg_norm_b,
           w_out, ln_g, ln_b):
    sg_bias = jnp.repeat(jnp.swapaxes(sg_b, 1, 2), B_HEAD_DIM, axis=2)
    params = (
        w_in.astype(jnp.bfloat16),
        conv_w,
        conv_b.reshape(DEPTH, 1, A_WIDTH),
        sg_w.astype(jnp.bfloat16),
        sg_bias,
        sg_norm_g.reshape(DEPTH, 1, B_WIDTH),
        sg_norm_b.reshape(DEPTH, 1, B_WIDTH),
        w_out.astype(jnp.bfloat16),
        ln_g.reshape(DEPTH, 1, D_MODEL),
        ln_b.reshape(DEPTH, 1, D_MODEL),
    )
    return (_trunk(x_prompt, params), _trunk(x_sample, params))
```

```python
import math

import jax
import jax.numpy as jnp
from jax import lax
from jax.experimental import pallas as pl
from jax.experimental.pallas import tpu as pltpu

D_MODEL = 1024
DEPTH = 4
A_WIDTH = 512
B_WIDTH = 512
B_HEADS = 4
B_HEAD_DIM = 128
CHUNK = 128
PROJ_OUT = 4 * A_WIDTH + 3 * B_WIDTH
ALPHA = (2.0 * DEPTH) ** 0.25
LN_EPS = 1e-5
GELU_C0 = math.sqrt(2.0 / math.pi)
GELU_C1 = 0.044715
LOG2_E = math.log2(math.e)

COL_A_IN = 0
COL_A_B = A_WIDTH
COL_A_C = 2 * A_WIDTH
COL_A_Z = 3 * A_WIDTH
COL_B_U = 4 * A_WIDTH
COL_B_V = 4 * A_WIDTH + B_WIDTH
COL_B_Z = 4 * A_WIDTH + 2 * B_WIDTH

HALO = 16
SEQ_TILE = 1024
ROW_BLOCK = 256
EXT = ROW_BLOCK + 2 * HALO
VMEM_LIMIT_BYTES = 56 * 1024 * 1024
F32 = jnp.float32
BF16 = jnp.bfloat16


def _gelu_tanh(x):
    k0 = -2.0 * GELU_C0 * LOG2_E
    return x * (1.0 / (1.0 + jnp.exp2(x * (k0 + (k0 * GELU_C1) * (x * x)))))


def _silu(x):
    return x * (1.0 / (1.0 + jnp.exp2(-LOG2_E * x)))


def _row_block(r0, x_ref, xb_ref, w_in_ref, conv_w_ref, conv_b_ref, sg_w_ref,
               sg_bias_ref, sg_g_ref, sg_b_ref, w_out_ref, ln_g_ref, ln_b_ref, o_ref):
    xe = xb_ref[r0:r0 + EXT, :]
    xm = xb_ref[r0 + HALO:r0 + HALO + ROW_BLOCK, :]

    b_v = jnp.dot(xm, w_in_ref[:, COL_B_V:COL_B_V + B_WIDTH], preferred_element_type=F32)
    v = _gelu_tanh(b_v)
    n_chunks = ROW_BLOCK // CHUNK
    rhs_heads = []
    for h in range(B_HEADS):
        vh = v[:, h * B_HEAD_DIM:(h + 1) * B_HEAD_DIM]
        mu = jnp.mean(vh, axis=-1, keepdims=True)
        d = vh - mu
        var = jnp.mean(d * d, axis=-1, keepdims=True)
        vn = (d * lax.rsqrt(var + LN_EPS)
              * sg_g_ref[:, h * B_HEAD_DIM:(h + 1) * B_HEAD_DIM]
              + sg_b_ref[:, h * B_HEAD_DIM:(h + 1) * B_HEAD_DIM]).astype(BF16)
        rhs_heads.append(jnp.concatenate(
            [vn[c * CHUNK:(c + 1) * CHUNK] for c in range(n_chunks)], axis=1))

    a_in = jnp.dot(xe, w_in_ref[:, COL_A_IN:COL_A_IN + A_WIDTH], preferred_element_type=F32)
    a_c = jnp.dot(xe, w_in_ref[:, COL_A_C:COL_A_C + A_WIDTH], preferred_element_type=F32)
    t = a_c * a_in
    t_m1 = pltpu.roll(t, 1, axis=0)[HALO:HALO + ROW_BLOCK]
    t_p1 = pltpu.roll(t, EXT - 1, axis=0)[HALO:HALO + ROW_BLOCK]
    t_0 = t[HALO:HALO + ROW_BLOCK]
    conv = (conv_w_ref[0:1, :] * t_m1 + conv_w_ref[1:2, :] * t_0
            + conv_w_ref[2:3, :] * t_p1 + conv_b_ref[...])
    a_b = jnp.dot(xm, w_in_ref[:, COL_A_B:COL_A_B + A_WIDTH], preferred_element_type=F32)
    a_z = jnp.dot(xm, w_in_ref[:, COL_A_Z:COL_A_Z + A_WIDTH], preferred_element_type=F32)
    out_a = (a_b * conv * _silu(a_z)).astype(BF16)

    s_heads = []
    for h in range(B_HEADS):
        s2 = jnp.dot(sg_w_ref[h], rhs_heads[h], preferred_element_type=F32)
        s_heads.append(jnp.concatenate(
            [s2[:, c * B_HEAD_DIM:(c + 1) * B_HEAD_DIM] for c in range(n_chunks)],
            axis=0))
    bias = sg_bias_ref[...]
    s = jnp.concatenate(s_heads, axis=1) + jnp.concatenate([bias] * n_chunks, axis=0)
    b_u = jnp.dot(xm, w_in_ref[:, COL_B_U:COL_B_U + B_WIDTH], preferred_element_type=F32)
    b_z = jnp.dot(xm, w_in_ref[:, COL_B_Z:COL_B_Z + B_WIDTH], preferred_element_type=F32)
    out_b = (_gelu_tanh(b_u) * s * _silu(b_z)).astype(BF16)

    mixed = jnp.concatenate([out_a, out_b], axis=1)
    y = jnp.dot(mixed, w_out_ref[...], preferred_element_type=F32)
    r = x_ref[0, r0:r0 + ROW_BLOCK, :] + y
    mu = jnp.mean(r, axis=-1, keepdims=True)
    d = r - mu
    var = jnp.mean(d * d, axis=-1, keepdims=True)
    o_ref[0, r0:r0 + ROW_BLOCK, :] = (
        d * lax.rsqrt(var + LN_EPS / (ALPHA * ALPHA)) * ln_g_ref[...] + ln_b_ref[...])


def _layer_kernel(x_ref, xprev_ref, xnext_ref, w_in_ref, conv_w_ref, conv_b_ref,
                  sg_w_ref, sg_bias_ref, sg_g_ref, sg_b_ref, w_out_ref,
                  ln_g_ref, ln_b_ref, o_ref, xb_ref):
    i = pl.program_id(1)
    n_i = pl.num_programs(1)

    prev = xprev_ref[0].astype(BF16)
    nxt = xnext_ref[0].astype(BF16)
    xb_ref[0:HALO, :] = jnp.where(i > 0, prev, jnp.zeros_like(prev))
    xb_ref[HALO + SEQ_TILE:, :] = jnp.where(i < n_i - 1, nxt, jnp.zeros_like(nxt))
    for r0 in range(0, SEQ_TILE, ROW_BLOCK):
        xb_ref[HALO + r0:HALO + r0 + ROW_BLOCK, :] = (
            x_ref[0, r0:r0 + ROW_BLOCK, :].astype(BF16))

    for r0 in range(0, SEQ_TILE, ROW_BLOCK):
        _row_block(r0, x_ref, xb_ref, w_in_ref, conv_w_ref, conv_b_ref, sg_w_ref,
                   sg_bias_ref, sg_g_ref, sg_b_ref, w_out_ref, ln_g_ref, ln_b_ref, o_ref)


def _layer(x, w_in, conv_w, conv_b, sg_w, sg_bias, sg_g, sg_b, w_out, ln_g, ln_b):
    bsz, seq, _ = x.shape
    assert seq % SEQ_TILE == 0 and SEQ_TILE % ROW_BLOCK == 0 and ROW_BLOCK % CHUNK == 0
    n_tiles = seq // SEQ_TILE
    halo_per_tile = SEQ_TILE // HALO
    n_halo_blocks = seq // HALO

    def const(shape):
        return pl.BlockSpec(shape, lambda b, i: (0,) * len(shape))

    in_specs = [
        pl.BlockSpec((1, SEQ_TILE, D_MODEL), lambda b, i: (b, i, 0)),
        pl.BlockSpec((1, HALO, D_MODEL),
                     lambda b, i: (b, jnp.maximum(i * halo_per_tile - 1, 0), 0)),
        pl.BlockSpec((1, HALO, D_MODEL),
                     lambda b, i: (b, jnp.minimum((i + 1) * halo_per_tile,
                                                  n_halo_blocks - 1), 0)),
        const((D_MODEL, PROJ_OUT)),
        const((3, A_WIDTH)),
        const((1, A_WIDTH)),
        const((B_HEADS, CHUNK, CHUNK)),
        const((CHUNK, B_WIDTH)),
        const((1, B_WIDTH)),
        const((1, B_WIDTH)),
        const((D_MODEL, D_MODEL)),
        const((1, D_MODEL)),
        const((1, D_MODEL)),
    ]
    return pl.pallas_call(
        _layer_kernel,
        grid=(bsz, n_tiles),
        in_specs=in_specs,
        out_specs=pl.BlockSpec((1, SEQ_TILE, D_MODEL), lambda b, i: (b, i, 0)),
        out_shape=jax.ShapeDtypeStruct(x.shape, jnp.float32),
        scratch_shapes=[pltpu.VMEM((SEQ_TILE + 2 * HALO, D_MODEL), BF16)],
        compiler_params=pltpu.CompilerParams(
            dimension_semantics=("arbitrary", "arbitrary"),
            vmem_limit_bytes=VMEM_LIMIT_BYTES),
        name="encoder_layer",
    )(x, x, x, w_in, conv_w, conv_b, sg_w, sg_bias, sg_g, sg_b, w_out, ln_g, ln_b)


def _trunk(x, params):
    for l in range(DEPTH):
        x = _layer(x, *(p[l] for p in params))
    return x


def kernel(x_prompt, x_sample, w_in, conv_w, conv_b, sg_w, sg_b, sg_norm_g, sg_norm_b,
           w_out, ln_g, ln_b):
    sg_bias = jnp.repeat(jnp.swapaxes(sg_b, 1, 2), B_HEAD_DIM, axis=2)
    params = (
        w_in.astype(BF16),
        conv_w,
        conv_b.reshape(DEPTH, 1, A_WIDTH),
        sg_w.astype(BF16),
        sg_bias,
        sg_norm_g.reshape(DEPTH, 1, B_WIDTH),
        sg_norm_b.reshape(DEPTH, 1, B_WIDTH),
        (w_out * (1.0 / ALPHA)).astype(BF16),
        ln_g.reshape(DEPTH, 1, D_MODEL),
        ln_b.reshape(DEPTH, 1, D_MODEL),
    )
    return (_trunk(x_prompt, params), _trunk(x_sample, params))
```

```python
import functools
import math

import jax
import jax.numpy as jnp
from jax import lax
from jax.experimental import pallas as pl
from jax.experimental.pallas import tpu as pltpu

D_MODEL = 1024
DEPTH = 4
A_WIDTH = 512
B_WIDTH = 512
B_HEADS = 4
B_HEAD_DIM = 128
CHUNK = 128
PROJ_OUT = 4 * A_WIDTH + 3 * B_WIDTH
ALPHA = (2.0 * DEPTH) ** 0.25
LN_EPS = 1e-5
GELU_C0 = math.sqrt(2.0 / math.pi)
GELU_C1 = 0.044715
LOG2_E = math.log2(math.e)

COL_A_IN = 0
COL_A_B = A_WIDTH
COL_A_C = 2 * A_WIDTH
COL_A_Z = 3 * A_WIDTH
COL_B_U = 4 * A_WIDTH
COL_B_V = 4 * A_WIDTH + B_WIDTH
COL_B_Z = 4 * A_WIDTH + 2 * B_WIDTH
COL_W_OUT = PROJ_OUT

HALO = 16
SEQ_TILE = 1024
ROW_BLOCK = 256
EXT = ROW_BLOCK + 2 * HALO
VMEM_LIMIT_BYTES = 56 * 1024 * 1024
F32 = jnp.float32
BF16 = jnp.bfloat16


def _gelu_tanh(x):
    k0 = -2.0 * GELU_C0 * LOG2_E
    return x * (1.0 / (1.0 + jnp.exp2(x * (k0 + (k0 * GELU_C1) * (x * x)))))


def _silu(x):
    return x * (1.0 / (1.0 + jnp.exp2(-LOG2_E * x)))


def _zero_after(x):
    u = pltpu.bitcast(x, jnp.int32)
    acc = u[:, 0:128]
    for c in range(1, x.shape[1] // 128):
        acc = acc | u[:, c * 128:(c + 1) * 128]
    out = acc[0:8]
    for g in range(1, x.shape[0] // 8):
        out = out | acc[g * 8:(g + 1) * 8]
    return lax.shift_right_logical(lax.shift_right_logical(out, 16), 16)


def _norm_block(r0, r_ref, ln_g_ref, ln_b_ref, o_ref):
    r = r_ref[r0:r0 + ROW_BLOCK, :]
    mu = jnp.mean(r, axis=-1, keepdims=True)
    d = r - mu
    var = jnp.mean(d * d, axis=-1, keepdims=True)
    out = d * lax.rsqrt(var + LN_EPS / (ALPHA * ALPHA)) * ln_g_ref[...] + ln_b_ref[...]
    o_ref[0, r0:r0 + ROW_BLOCK, :] = out
    return out


def _row_block(r0, x_ref, xb_ref, w_ref, conv_w_ref, conv_b_ref, sg_w_ref,
               sg_bias_ref, sg_g_ref, sg_b_ref, r_ref, normed=None):
    xe = xb_ref[r0:r0 + EXT, :]
    xm = xb_ref[r0 + HALO:r0 + HALO + ROW_BLOCK, :]

    b_v = jnp.dot(xm, w_ref[:, COL_B_V:COL_B_V + B_WIDTH], preferred_element_type=F32)
    v = _gelu_tanh(b_v)
    n_chunks = ROW_BLOCK // CHUNK
    rhs_heads = []
    for h in range(B_HEADS):
        vh = v[:, h * B_HEAD_DIM:(h + 1) * B_HEAD_DIM]
        mu = jnp.mean(vh, axis=-1, keepdims=True)
        d = vh - mu
        var = jnp.mean(d * d, axis=-1, keepdims=True)
        vn = (d * lax.rsqrt(var + LN_EPS)
              * sg_g_ref[:, h * B_HEAD_DIM:(h + 1) * B_HEAD_DIM]
              + sg_b_ref[:, h * B_HEAD_DIM:(h + 1) * B_HEAD_DIM]).astype(BF16)
        rhs_heads.append(jnp.concatenate(
            [vn[c * CHUNK:(c + 1) * CHUNK] for c in range(n_chunks)], axis=1))

    a_in = jnp.dot(xe, w_ref[:, COL_A_IN:COL_A_IN + A_WIDTH], preferred_element_type=F32)
    a_c = jnp.dot(xe, w_ref[:, COL_A_C:COL_A_C + A_WIDTH], preferred_element_type=F32)
    t = a_c * a_in
    t_m1 = pltpu.roll(t, 1, axis=0)[HALO:HALO + ROW_BLOCK]
    t_p1 = pltpu.roll(t, EXT - 1, axis=0)[HALO:HALO + ROW_BLOCK]
    t_0 = t[HALO:HALO + ROW_BLOCK]
    conv = (conv_w_ref[0:1, :] * t_m1 + conv_w_ref[1:2, :] * t_0
            + conv_w_ref[2:3, :] * t_p1 + conv_b_ref[...])
    w_ab = w_ref[:, COL_A_B:COL_A_B + A_WIDTH]
    if normed is not None:
        corner = pltpu.bitcast(w_ab[0:HALO, 0:128], jnp.int32) | _zero_after(normed)
        top = jnp.concatenate([pltpu.bitcast(corner, BF16), w_ab[0:HALO, 128:]], axis=1)
        w_ab = jnp.concatenate([top, w_ab[HALO:]], axis=0)
    a_b = jnp.dot(xm, w_ab, preferred_element_type=F32)
    a_z = jnp.dot(xm, w_ref[:, COL_A_Z:COL_A_Z + A_WIDTH], preferred_element_type=F32)
    out_a = (a_b * conv * _silu(a_z)).astype(BF16)

    s_heads = []
    for h in range(B_HEADS):
        s2 = jnp.dot(sg_w_ref[h], rhs_heads[h], preferred_element_type=F32)
        s_heads.append(jnp.concatenate(
            [s2[:, c * B_HEAD_DIM:(c + 1) * B_HEAD_DIM] for c in range(n_chunks)],
            axis=0))
    bias = sg_bias_ref[...]
    s = jnp.concatenate(s_heads, axis=1) + jnp.concatenate([bias] * n_chunks, axis=0)
    b_u = jnp.dot(xm, w_ref[:, COL_B_U:COL_B_U + B_WIDTH], preferred_element_type=F32)
    b_z = jnp.dot(xm, w_ref[:, COL_B_Z:COL_B_Z + B_WIDTH], preferred_element_type=F32)
    out_b = (_gelu_tanh(b_u) * s * _silu(b_z)).astype(BF16)

    mixed = jnp.concatenate([out_a, out_b], axis=1)
    y = jnp.dot(mixed, w_ref[:, COL_W_OUT:COL_W_OUT + D_MODEL], preferred_element_type=F32)
    r_ref[r0:r0 + ROW_BLOCK, :] = x_ref[0, r0:r0 + ROW_BLOCK, :] + y


def _layer_kernel(n_tiles, x_ref, xprev_ref, xnext_ref, w_ref, conv_w_ref, conv_b_ref,
                  sg_w_ref, sg_bias_ref, sg_g_ref, sg_b_ref, ln_g_ref, ln_b_ref,
                  o_ref, xb_ref, r_ref):
    s = pl.program_id(0)
    n_slabs = pl.num_programs(0) - 1
    i = lax.rem(jnp.minimum(s, n_slabs - 1), n_tiles)

    def stage_in():
        prev = xprev_ref[0].astype(BF16)
        nxt = xnext_ref[0].astype(BF16)
        xb_ref[0:HALO, :] = jnp.where(i > 0, prev, jnp.zeros_like(prev))
        xb_ref[HALO + SEQ_TILE:, :] = jnp.where(i < n_tiles - 1, nxt, jnp.zeros_like(nxt))
        for r0 in range(0, SEQ_TILE, ROW_BLOCK):
            xb_ref[HALO + r0:HALO + r0 + ROW_BLOCK, :] = (
                x_ref[0, r0:r0 + ROW_BLOCK, :].astype(BF16))

    def norm(r0):
        return _norm_block(r0, r_ref, ln_g_ref, ln_b_ref, o_ref)

    def rows(r0, normed=None):
        _row_block(r0, x_ref, xb_ref, w_ref, conv_w_ref, conv_b_ref, sg_w_ref,
                   sg_bias_ref, sg_g_ref, sg_b_ref, r_ref, normed)

    blocks = range(0, SEQ_TILE, ROW_BLOCK)

    @pl.when(s == 0)
    def _():
        stage_in()
        for r0 in blocks:
            rows(r0)

    @pl.when(jnp.logical_and(s > 0, s < n_slabs))
    def _():
        stage_in()
        for r0 in blocks:
            rows(r0, norm(r0))

    @pl.when(s == n_slabs)
    def _():
        for r0 in blocks:
            norm(r0)


def _layer(x, w_all, conv_w, conv_b, sg_w, sg_bias, sg_g, sg_b, ln_g, ln_b):
    bsz, seq, _ = x.shape
    assert seq % SEQ_TILE == 0 and SEQ_TILE % ROW_BLOCK == 0 and ROW_BLOCK % CHUNK == 0
    n_tiles = seq // SEQ_TILE
    n_slabs = bsz * n_tiles
    halo_per_tile = SEQ_TILE // HALO
    n_halo_blocks = seq // HALO

    def slab(s):
        t = jnp.clip(s, 0, n_slabs - 1)
        return t // n_tiles, t % n_tiles

    def x_map(s):
        b, i = slab(s)
        return b, i, 0

    def prev_map(s):
        b, i = slab(s)
        return b, jnp.maximum(i * halo_per_tile - 1, 0), 0

    def next_map(s):
        b, i = slab(s)
        return b, jnp.minimum((i + 1) * halo_per_tile, n_halo_blocks - 1), 0

    def out_map(s):
        b, i = slab(s - 1)
        return b, i, 0

    def const(shape):
        return pl.BlockSpec(shape, lambda s: (0,) * len(shape))

    in_specs = [
        pl.BlockSpec((1, SEQ_TILE, D_MODEL), x_map),
        pl.BlockSpec((1, HALO, D_MODEL), prev_map),
        pl.BlockSpec((1, HALO, D_MODEL), next_map),
        const((D_MODEL, PROJ_OUT + D_MODEL)),
        const((3, A_WIDTH)),
        const((1, A_WIDTH)),
        const((B_HEADS, CHUNK, CHUNK)),
        const((CHUNK, B_WIDTH)),
        const((1, B_WIDTH)),
        const((1, B_WIDTH)),
        const((1, D_MODEL)),
        const((1, D_MODEL)),
    ]
    return pl.pallas_call(
        functools.partial(_layer_kernel, n_tiles),
        grid=(n_slabs + 1,),
        in_specs=in_specs,
        out_specs=pl.BlockSpec((1, SEQ_TILE, D_MODEL), out_map),
        out_shape=jax.ShapeDtypeStruct(x.shape, jnp.float32),
        scratch_shapes=[pltpu.VMEM((SEQ_TILE + 2 * HALO, D_MODEL), BF16),
                        pltpu.VMEM((SEQ_TILE, D_MODEL), F32)],
        compiler_params=pltpu.CompilerParams(
            dimension_semantics=("arbitrary",),
            vmem_limit_bytes=VMEM_LIMIT_BYTES),
        name="encoder_layer",
    )(x, x, x, w_all, conv_w, conv_b, sg_w, sg_bias, sg_g, sg_b, ln_g, ln_b)


def _trunk(x, params):
    for l in range(DEPTH):
        x = _layer(x, *(p[l] for p in params))
    return x


def kernel(x_prompt, x_sample, w_in, conv_w, conv_b, sg_w, sg_b, sg_norm_g, sg_norm_b,
           w_out, ln_g, ln_b):
    sg_bias = jnp.repeat(jnp.swapaxes(sg_b, 1, 2), B_HEAD_DIM, axis=2)
    w_all = jnp.concatenate([w_in.astype(BF16), (w_out * (1.0 / ALPHA)).astype(BF16)], axis=2)
    params = (
        w_all,
        conv_w,
        conv_b.reshape(DEPTH, 1, A_WIDTH),
        sg_w.astype(BF16),
        sg_bias,
        sg_norm_g.reshape(DEPTH, 1, B_WIDTH),
        sg_norm_b.reshape(DEPTH, 1, B_WIDTH),
        ln_g.reshape(DEPTH, 1, D_MODEL),
        ln_b.reshape(DEPTH, 1, D_MODEL),
    )
    return (_trunk(x_prompt, params), _trunk(x_sample, params))
```

```python
import functools
import math

import jax
import jax.numpy as jnp
from jax import lax
from jax.experimental import pallas as pl
from jax.experimental.pallas import tpu as pltpu

D_MODEL = 1024
DEPTH = 4
A_WIDTH = 512
B_WIDTH = 512
B_HEADS = 4
B_HEAD_DIM = 128
CHUNK = 128
PROJ_OUT = 4 * A_WIDTH + 3 * B_WIDTH
ALPHA = (2.0 * DEPTH) ** 0.25
LN_EPS = 1e-5
GELU_C0 = math.sqrt(2.0 / math.pi)
GELU_C1 = 0.044715
LOG2_E = math.log2(math.e)

COL_A_IN = 0
COL_A_B = A_WIDTH
COL_A_C = 2 * A_WIDTH
COL_A_Z = 3 * A_WIDTH
COL_B_U = 4 * A_WIDTH
COL_B_V = 4 * A_WIDTH + B_WIDTH
COL_B_Z = 4 * A_WIDTH + 2 * B_WIDTH
COL_W_OUT = PROJ_OUT

HALO = 16
SEQ_TILE = 1024
ROW_BLOCK = 256
N_BLOCKS = SEQ_TILE // ROW_BLOCK
VMEM_LIMIT_BYTES = 56 * 1024 * 1024
F32 = jnp.float32
BF16 = jnp.bfloat16


def _gelu_tanh(x):
    k0 = -2.0 * GELU_C0 * LOG2_E
    return x * (1.0 / (1.0 + jnp.exp2(x * (k0 + (k0 * GELU_C1) * (x * x)))))


def _silu(x):
    return x * (1.0 / (1.0 + jnp.exp2(-LOG2_E * x)))


def _zero_after(x):
    u = pltpu.bitcast(x, jnp.int32)
    acc = u[:, 0:128]
    for c in range(1, x.shape[1] // 128):
        acc = acc | u[:, c * 128:(c + 1) * 128]
    out = acc[0:8]
    for g in range(1, x.shape[0] // 8):
        out = out | acc[g * 8:(g + 1) * 8]
    return lax.shift_right_logical(lax.shift_right_logical(out, 16), 16)


def _ordered_after(w, x):
    if x is None:
        return w
    zero = pltpu.bitcast(_zero_after(x), F32)
    corner = w[0:HALO, 0:128].astype(F32) + jnp.concatenate([zero] * (HALO // 8), axis=0)
    top = jnp.concatenate([corner.astype(BF16), w[0:HALO, 128:]], axis=1)
    return jnp.concatenate([top, w[HALO:]], axis=0)


def _norm_block(r0, r_ref, ln_g_ref, ln_b_ref, o_ref):
    r = r_ref[r0:r0 + ROW_BLOCK, :]
    mu = jnp.mean(r, axis=-1, keepdims=True)
    d = r - mu
    var = jnp.mean(d * d, axis=-1, keepdims=True)
    out = d * lax.rsqrt(var + LN_EPS / (ALPHA * ALPHA)) * ln_g_ref[...] + ln_b_ref[...]
    o_ref[0, r0:r0 + ROW_BLOCK, :] = out
    return out


def _front_block(j, xb_ref, w_ref, sg_w_ref, sg_g_ref, sg_b_ref):
    r0 = j * ROW_BLOCK
    lo = 0 if j == 0 else HALO + r0
    hi = SEQ_TILE + 2 * HALO if j == N_BLOCKS - 1 else HALO + r0 + ROW_BLOCK
    xm = xb_ref[r0 + HALO:r0 + HALO + ROW_BLOCK, :]
    xa = xb_ref[lo:hi, :]

    b_v = jnp.dot(xm, w_ref[:, COL_B_V:COL_B_V + B_WIDTH], preferred_element_type=F32)
    v = _gelu_tanh(b_v)
    n_chunks = ROW_BLOCK // CHUNK
    rhs_heads = []
    for h in range(B_HEADS):
        vh = v[:, h * B_HEAD_DIM:(h + 1) * B_HEAD_DIM]
        mu = jnp.mean(vh, axis=-1, keepdims=True)
        d = vh - mu
        var = jnp.mean(d * d, axis=-1, keepdims=True)
        vn = (d * lax.rsqrt(var + LN_EPS)
              * sg_g_ref[:, h * B_HEAD_DIM:(h + 1) * B_HEAD_DIM]
              + sg_b_ref[:, h * B_HEAD_DIM:(h + 1) * B_HEAD_DIM]).astype(BF16)
        rhs_heads.append(jnp.concatenate(
            [vn[c * CHUNK:(c + 1) * CHUNK] for c in range(n_chunks)], axis=1))

    a_in = jnp.dot(xa, w_ref[:, COL_A_IN:COL_A_IN + A_WIDTH], preferred_element_type=F32)
    a_c = jnp.dot(xa, w_ref[:, COL_A_C:COL_A_C + A_WIDTH], preferred_element_type=F32)

    s_heads = []
    for h in range(B_HEADS):
        s2 = jnp.dot(sg_w_ref[h], rhs_heads[h], preferred_element_type=F32)
        s_heads.append(jnp.concatenate(
            [s2[:, c * B_HEAD_DIM:(c + 1) * B_HEAD_DIM] for c in range(n_chunks)],
            axis=0))
    return jnp.concatenate(s_heads, axis=1), (lo, a_c * a_in)


def _t_rows(pieces, a, b):
    for lo, t in pieces:
        if lo <= a and b <= lo + t.shape[0]:
            return t[a - lo:b - lo]
    raise ValueError("rows span two pieces")


def _back_block(j, s, pieces, x_ref, xb_ref, w_ref, conv_w_ref, conv_b_ref,
                sg_bias_ref, r_ref, normed=None):
    r0 = j * ROW_BLOCK
    xm = xb_ref[r0 + HALO:r0 + HALO + ROW_BLOCK, :]
    n_chunks = ROW_BLOCK // CHUNK

    first = HALO + r0
    t_0 = _t_rows(pieces, first, first + ROW_BLOCK)
    t_ext = jnp.concatenate([_t_rows(pieces, first - 8, first), t_0,
                             _t_rows(pieces, first + ROW_BLOCK, first + ROW_BLOCK + 8)], axis=0)
    t_m1 = pltpu.roll(t_ext, 1, axis=0)[8:8 + ROW_BLOCK]
    t_p1 = pltpu.roll(t_ext, ROW_BLOCK + 15, axis=0)[8:8 + ROW_BLOCK]
    conv = (conv_w_ref[0:1, :] * t_m1 + conv_w_ref[1:2, :] * t_0
            + conv_w_ref[2:3, :] * t_p1 + conv_b_ref[...])
    quarters = [None] * 4 if normed is None else [
        normed[q * (ROW_BLOCK // 4):(q + 1) * (ROW_BLOCK // 4)] for q in range(4)]
    a_b = jnp.dot(xm, _ordered_after(w_ref[:, COL_A_B:COL_A_B + A_WIDTH], quarters[0]),
                  preferred_element_type=F32)
    a_z = jnp.dot(xm, _ordered_after(w_ref[:, COL_A_Z:COL_A_Z + A_WIDTH], quarters[1]),
                  preferred_element_type=F32)
    out_a = (a_b * conv * _silu(a_z)).astype(BF16)

    s = s + jnp.concatenate([sg_bias_ref[...]] * n_chunks, axis=0)
    b_u = jnp.dot(xm, _ordered_after(w_ref[:, COL_B_U:COL_B_U + B_WIDTH], quarters[2]),
                  preferred_element_type=F32)
    b_z = jnp.dot(xm, _ordered_after(w_ref[:, COL_B_Z:COL_B_Z + B_WIDTH], quarters[3]),
                  preferred_element_type=F32)
    out_b = (_gelu_tanh(b_u) * s * _silu(b_z)).astype(BF16)

    mixed = jnp.concatenate([out_a, out_b], axis=1)
    y = jnp.dot(mixed, w_ref[:, COL_W_OUT:COL_W_OUT + D_MODEL], preferred_element_type=F32)
    r_ref[r0:r0 + ROW_BLOCK, :] = x_ref[0, r0:r0 + ROW_BLOCK, :] + y


def _layer_kernel(n_tiles, x_ref, xprev_ref, xnext_ref, w_ref, conv_w_ref, conv_b_ref,
                  sg_w_ref, sg_bias_ref, sg_g_ref, sg_b_ref, ln_g_ref, ln_b_ref,
                  o_ref, xb_ref, r_ref):
    s = pl.program_id(0)
    n_slabs = pl.num_programs(0) - 1
    i = lax.rem(jnp.minimum(s, n_slabs - 1), n_tiles)

    def stage_in():
        prev = xprev_ref[0].astype(BF16)
        nxt = xnext_ref[0].astype(BF16)
        xb_ref[0:HALO, :] = jnp.where(i > 0, prev, jnp.zeros_like(prev))
        xb_ref[HALO + SEQ_TILE:, :] = jnp.where(i < n_tiles - 1, nxt, jnp.zeros_like(nxt))
        for r0 in range(0, SEQ_TILE, ROW_BLOCK):
            xb_ref[HALO + r0:HALO + r0 + ROW_BLOCK, :] = (
                x_ref[0, r0:r0 + ROW_BLOCK, :].astype(BF16))

    def norm(j):
        return _norm_block(j * ROW_BLOCK, r_ref, ln_g_ref, ln_b_ref, o_ref)

    def slab_rows(with_norm):
        def front(j):
            return _front_block(j, xb_ref, w_ref, sg_w_ref, sg_g_ref, sg_b_ref)

        fronts = {0: front(0)}
        for j in range(N_BLOCKS):
            normed = norm(j) if with_norm else None
            if j + 1 < N_BLOCKS:
                fronts[j + 1] = front(j + 1)
            pieces = [fronts[k][1] for k in (j - 1, j, j + 1) if k in fronts]
            _back_block(j, fronts[j][0], pieces, x_ref, xb_ref, w_ref, conv_w_ref, conv_b_ref,
                        sg_bias_ref, r_ref, normed)
            fronts.pop(j - 1, None)

    @pl.when(s == 0)
    def _():
        stage_in()
        slab_rows(False)

    @pl.when(jnp.logical_and(s > 0, s < n_slabs))
    def _():
        stage_in()
        slab_rows(True)

    @pl.when(s == n_slabs)
    def _():
        for j in range(N_BLOCKS):
            norm(j)


def _layer(x, w_all, conv_w, conv_b, sg_w, sg_bias, sg_g, sg_b, ln_g, ln_b):
    bsz, seq, _ = x.shape
    assert seq % SEQ_TILE == 0 and SEQ_TILE % ROW_BLOCK == 0 and ROW_BLOCK % CHUNK == 0
    n_tiles = seq // SEQ_TILE
    n_slabs = bsz * n_tiles
    halo_per_tile = SEQ_TILE // HALO
    n_halo_blocks = seq // HALO

    def slab(s):
        t = jnp.clip(s, 0, n_slabs - 1)
        return t // n_tiles, t % n_tiles

    def x_map(s):
        b, i = slab(s)
        return b, i, 0

    def prev_map(s):
        b, i = slab(s)
        return b, jnp.maximum(i * halo_per_tile - 1, 0), 0

    def next_map(s):
        b, i = slab(s)
        return b, jnp.minimum((i + 1) * halo_per_tile, n_halo_blocks - 1), 0

    def out_map(s):
        b, i = slab(s - 1)
        return b, i, 0

    def const(shape):
        return pl.BlockSpec(shape, lambda s: (0,) * len(shape))

    in_specs = [
        pl.BlockSpec((1, SEQ_TILE, D_MODEL), x_map),
        pl.BlockSpec((1, HALO, D_MODEL), prev_map),
        pl.BlockSpec((1, HALO, D_MODEL), next_map),
        const((D_MODEL, PROJ_OUT + D_MODEL)),
        const((3, A_WIDTH)),
        const((1, A_WIDTH)),
        const((B_HEADS, CHUNK, CHUNK)),
        const((CHUNK, B_WIDTH)),
        const((1, B_WIDTH)),
        const((1, B_WIDTH)),
        const((1, D_MODEL)),
        const((1, D_MODEL)),
    ]
    return pl.pallas_call(
        functools.partial(_layer_kernel, n_tiles),
        grid=(n_slabs + 1,),
        in_specs=in_specs,
        out_specs=pl.BlockSpec((1, SEQ_TILE, D_MODEL), out_map),
        out_shape=jax.ShapeDtypeStruct(x.shape, jnp.float32),
        scratch_shapes=[pltpu.VMEM((SEQ_TILE + 2 * HALO, D_MODEL), BF16),
                        pltpu.VMEM((SEQ_TILE, D_MODEL), F32)],
        compiler_params=pltpu.CompilerParams(
            dimension_semantics=("arbitrary",),
            vmem_limit_bytes=VMEM_LIMIT_BYTES),
        name="encoder_layer",
    )(x, x, x, w_all, conv_w, conv_b, sg_w, sg_bias, sg_g, sg_b, ln_g, ln_b)


def _trunk(x, params):
    for l in range(DEPTH):
        x = _layer(x, *(p[l] for p in params))
    return x


def kernel(x_prompt, x_sample, w_in, conv_w, conv_b, sg_w, sg_b, sg_norm_g, sg_norm_b,
           w_out, ln_g, ln_b):
    sg_bias = jnp.repeat(jnp.swapaxes(sg_b, 1, 2), B_HEAD_DIM, axis=2)
    w_all = jnp.concatenate([w_in, w_out * (1.0 / ALPHA)], axis=2).astype(BF16)
    params = (
        w_all,
        conv_w,
        conv_b.reshape(DEPTH, 1, A_WIDTH),
        sg_w.astype(BF16),
        sg_bias,
        sg_norm_g.reshape(DEPTH, 1, B_WIDTH),
        sg_norm_b.reshape(DEPTH, 1, B_WIDTH),
        ln_g.reshape(DEPTH, 1, D_MODEL),
        ln_b.reshape(DEPTH, 1, D_MODEL),
    )
    return (_trunk(x_prompt, params), _trunk(x_sample, params))
```

```python
import functools
import math

import jax
import jax.numpy as jnp
from jax import lax
from jax.experimental import pallas as pl
from jax.experimental.pallas import tpu as pltpu

D_MODEL = 1024
DEPTH = 4
A_WIDTH = 512
B_WIDTH = 512
B_HEADS = 4
B_HEAD_DIM = 128
CHUNK = 128
PROJ_OUT = 4 * A_WIDTH + 3 * B_WIDTH
ALPHA = (2.0 * DEPTH) ** 0.25
LN_EPS = 1e-5
GELU_C0 = math.sqrt(2.0 / math.pi)
GELU_C1 = 0.044715
LOG2_E = math.log2(math.e)

COL_A_IN = 0
COL_A_B = A_WIDTH
COL_A_C = 2 * A_WIDTH
COL_A_Z = 3 * A_WIDTH
COL_B_U = 4 * A_WIDTH
COL_B_V = 4 * A_WIDTH + B_WIDTH
COL_B_Z = 4 * A_WIDTH + 2 * B_WIDTH
W_OUT_SPARE_COLS = 128

HALO = 16
SEQ_TILE = 1024
ROW_BLOCK = 256
N_BLOCKS = SEQ_TILE // ROW_BLOCK
VMEM_LIMIT_BYTES = 56 * 1024 * 1024
F32 = jnp.float32
BF16 = jnp.bfloat16


def _gelu_tanh(x):
    k0 = -2.0 * GELU_C0 * LOG2_E
    return x * (1.0 / (1.0 + jnp.exp2(x * (k0 + (k0 * GELU_C1) * (x * x)))))


def _silu(x):
    return x * (1.0 / (1.0 + jnp.exp2(-LOG2_E * x)))


def _zero_after(x):
    u = pltpu.bitcast(x, jnp.int32)
    acc = u[:, 0:128]
    for c in range(1, x.shape[1] // 128):
        acc = acc | u[:, c * 128:(c + 1) * 128]
    out = acc[0:8]
    for g in range(1, x.shape[0] // 8):
        out = out | acc[g * 8:(g + 1) * 8]
    return lax.shift_right_logical(lax.shift_right_logical(out, 16), 16)


def _ordered_after(w, x):
    if x is None:
        return w
    zero = pltpu.bitcast(_zero_after(x), F32)
    corner = w[0:HALO, 0:128].astype(F32) + jnp.concatenate([zero] * (HALO // 8), axis=0)
    top = jnp.concatenate([corner.astype(BF16), w[0:HALO, 128:]], axis=1)
    return jnp.concatenate([top, w[HALO:]], axis=0)


def _norm_block(r0, r_ref, ln_g_ref, ln_b_ref, o_ref):
    r = r_ref[r0:r0 + ROW_BLOCK, :]
    mu = jnp.mean(r, axis=-1, keepdims=True)
    d = r - mu
    var = jnp.mean(d * d, axis=-1, keepdims=True)
    out = d * lax.rsqrt(var + LN_EPS / (ALPHA * ALPHA)) * ln_g_ref[...] + ln_b_ref[...]
    o_ref[0, r0:r0 + ROW_BLOCK, :] = out
    return out


def _front_block(j, xb_ref, w_ref, sg_w_ref, sg_g_ref, sg_b_ref):
    r0 = j * ROW_BLOCK
    lo = 0 if j == 0 else HALO + r0
    hi = SEQ_TILE + 2 * HALO if j == N_BLOCKS - 1 else HALO + r0 + ROW_BLOCK
    xm = xb_ref[r0 + HALO:r0 + HALO + ROW_BLOCK, :]
    xa = xb_ref[lo:hi, :]

    b_v = jnp.dot(xm, w_ref[:, COL_B_V:COL_B_V + B_WIDTH], preferred_element_type=F32)
    v = _gelu_tanh(b_v)
    n_chunks = ROW_BLOCK // CHUNK
    rhs_heads = []
    for h in range(B_HEADS):
        vh = v[:, h * B_HEAD_DIM:(h + 1) * B_HEAD_DIM]
        mu = jnp.mean(vh, axis=-1, keepdims=True)
        d = vh - mu
        var = jnp.mean(d * d, axis=-1, keepdims=True)
        vn = (d * lax.rsqrt(var + LN_EPS)
              * sg_g_ref[:, h * B_HEAD_DIM:(h + 1) * B_HEAD_DIM]
              + sg_b_ref[:, h * B_HEAD_DIM:(h + 1) * B_HEAD_DIM]).astype(BF16)
        rhs_heads.append(jnp.concatenate(
            [vn[c * CHUNK:(c + 1) * CHUNK] for c in range(n_chunks)], axis=1))

    a_in = jnp.dot(xa, w_ref[:, COL_A_IN:COL_A_IN + A_WIDTH], preferred_element_type=F32)
    a_c = jnp.dot(xa, w_ref[:, COL_A_C:COL_A_C + A_WIDTH], preferred_element_type=F32)

    s_heads = []
    for h in range(B_HEADS):
        s2 = jnp.dot(sg_w_ref[h], rhs_heads[h], preferred_element_type=F32)
        s_heads.append(jnp.concatenate(
            [s2[:, c * B_HEAD_DIM:(c + 1) * B_HEAD_DIM] for c in range(n_chunks)],
            axis=0))
    return jnp.concatenate(s_heads, axis=1), (lo, a_c * a_in)


def _t_rows(pieces, a, b):
    for lo, t in pieces:
        if lo <= a and b <= lo + t.shape[0]:
            return t[a - lo:b - lo]
    raise ValueError("rows span two pieces")


def _back_block(j, s, pieces, x_ref, xb_ref, w_ref, w_out_ref, conv_w_ref, conv_b_ref,
                sg_bias_ref, r_ref, normed=None):
    r0 = j * ROW_BLOCK
    xm = xb_ref[r0 + HALO:r0 + HALO + ROW_BLOCK, :]
    n_chunks = ROW_BLOCK // CHUNK

    first = HALO + r0
    t_0 = _t_rows(pieces, first, first + ROW_BLOCK)
    t_ext = jnp.concatenate([_t_rows(pieces, first - 8, first), t_0,
                             _t_rows(pieces, first + ROW_BLOCK, first + ROW_BLOCK + 8)], axis=0)
    t_m1 = pltpu.roll(t_ext, 1, axis=0)[8:8 + ROW_BLOCK]
    t_p1 = pltpu.roll(t_ext, ROW_BLOCK + 15, axis=0)[8:8 + ROW_BLOCK]
    conv = (conv_w_ref[0:1, :] * t_m1 + conv_w_ref[1:2, :] * t_0
            + conv_w_ref[2:3, :] * t_p1 + conv_b_ref[...])
    quarters = [None] * 4 if normed is None else [
        normed[q * (ROW_BLOCK // 4):(q + 1) * (ROW_BLOCK // 4)] for q in range(4)]
    a_b = jnp.dot(xm, _ordered_after(w_ref[:, COL_A_B:COL_A_B + A_WIDTH], quarters[0]),
                  preferred_element_type=F32)
    a_z = jnp.dot(xm, _ordered_after(w_ref[:, COL_A_Z:COL_A_Z + A_WIDTH], quarters[1]),
                  preferred_element_type=F32)
    out_a = (a_b * conv * _silu(a_z)).astype(BF16)

    s = s + jnp.concatenate([sg_bias_ref[...]] * n_chunks, axis=0)
    b_u = jnp.dot(xm, _ordered_after(w_ref[:, COL_B_U:COL_B_U + B_WIDTH], quarters[2]),
                  preferred_element_type=F32)
    b_z = jnp.dot(xm, _ordered_after(w_ref[:, COL_B_Z:COL_B_Z + B_WIDTH], quarters[3]),
                  preferred_element_type=F32)
    out_b = (_gelu_tanh(b_u) * s * _silu(b_z)).astype(BF16)

    mixed = jnp.concatenate([out_a, out_b], axis=1)
    y = jnp.dot(mixed, w_out_ref[:, 0:D_MODEL], preferred_element_type=F32)
    r_ref[r0:r0 + ROW_BLOCK, :] = x_ref[0, r0:r0 + ROW_BLOCK, :] + y


def _layer_kernel(n_tiles, x_ref, xprev_ref, xnext_ref, w_ref, w_out_ref, conv_w_ref, conv_b_ref,
                  sg_w_ref, sg_bias_ref, sg_g_ref, sg_b_ref, ln_g_ref, ln_b_ref,
                  o_ref, xb_ref, r_ref):
    s = pl.program_id(0)
    n_slabs = pl.num_programs(0) - 1
    i = lax.rem(jnp.minimum(s, n_slabs - 1), n_tiles)

    def stage_in():
        prev = xprev_ref[0].astype(BF16)
        nxt = xnext_ref[0].astype(BF16)
        xb_ref[0:HALO, :] = jnp.where(i > 0, prev, jnp.zeros_like(prev))
        xb_ref[HALO + SEQ_TILE:, :] = jnp.where(i < n_tiles - 1, nxt, jnp.zeros_like(nxt))
        for r0 in range(0, SEQ_TILE, ROW_BLOCK):
            xb_ref[HALO + r0:HALO + r0 + ROW_BLOCK, :] = (
                x_ref[0, r0:r0 + ROW_BLOCK, :].astype(BF16))

    def norm(j):
        return _norm_block(j * ROW_BLOCK, r_ref, ln_g_ref, ln_b_ref, o_ref)

    def slab_rows(with_norm):
        def front(j):
            return _front_block(j, xb_ref, w_ref, sg_w_ref, sg_g_ref, sg_b_ref)

        fronts = {0: front(0)}
        for j in range(N_BLOCKS):
            normed = norm(j) if with_norm else None
            if j + 1 < N_BLOCKS:
                fronts[j + 1] = front(j + 1)
            pieces = [fronts[k][1] for k in (j - 1, j, j + 1) if k in fronts]
            _back_block(j, fronts[j][0], pieces, x_ref, xb_ref, w_ref, w_out_ref, conv_w_ref, conv_b_ref,
                        sg_bias_ref, r_ref, normed)
            fronts.pop(j - 1, None)

    @pl.when(s == 0)
    def _():
        stage_in()
        slab_rows(False)

    @pl.when(jnp.logical_and(s > 0, s < n_slabs))
    def _():
        stage_in()
        slab_rows(True)

    @pl.when(s == n_slabs)
    def _():
        for j in range(N_BLOCKS):
            norm(j)


def _layer(x, w_in, w_out, conv_w, conv_b, sg_w, sg_bias, sg_g, sg_b, ln_g, ln_b):
    bsz, seq, _ = x.shape
    assert seq % SEQ_TILE == 0 and SEQ_TILE % ROW_BLOCK == 0 and ROW_BLOCK % CHUNK == 0
    n_tiles = seq // SEQ_TILE
    n_slabs = bsz * n_tiles
    halo_per_tile = SEQ_TILE // HALO
    n_halo_blocks = seq // HALO

    def slab(s):
        t = jnp.clip(s, 0, n_slabs - 1)
        return t // n_tiles, t % n_tiles

    def x_map(s):
        b, i = slab(s)
        return b, i, 0

    def prev_map(s):
        b, i = slab(s)
        return b, jnp.maximum(i * halo_per_tile - 1, 0), 0

    def next_map(s):
        b, i = slab(s)
        return b, jnp.minimum((i + 1) * halo_per_tile, n_halo_blocks - 1), 0

    def out_map(s):
        b, i = slab(s - 1)
        return b, i, 0

    def const(shape):
        return pl.BlockSpec(shape, lambda s: (0,) * len(shape))

    in_specs = [
        pl.BlockSpec((1, SEQ_TILE, D_MODEL), x_map),
        pl.BlockSpec((1, HALO, D_MODEL), prev_map),
        pl.BlockSpec((1, HALO, D_MODEL), next_map),
        const((D_MODEL, PROJ_OUT)),
        const((D_MODEL, D_MODEL + W_OUT_SPARE_COLS)),
        const((3, A_WIDTH)),
        const((1, A_WIDTH)),
        const((B_HEADS, CHUNK, CHUNK)),
        const((CHUNK, B_WIDTH)),
        const((1, B_WIDTH)),
        const((1, B_WIDTH)),
        const((1, D_MODEL)),
        const((1, D_MODEL)),
    ]
    return pl.pallas_call(
        functools.partial(_layer_kernel, n_tiles),
        grid=(n_slabs + 1,),
        in_specs=in_specs,
        out_specs=pl.BlockSpec((1, SEQ_TILE, D_MODEL), out_map),
        out_shape=jax.ShapeDtypeStruct(x.shape, jnp.float32),
        scratch_shapes=[pltpu.VMEM((SEQ_TILE + 2 * HALO, D_MODEL), BF16),
                        pltpu.VMEM((SEQ_TILE, D_MODEL), F32)],
        compiler_params=pltpu.CompilerParams(
            dimension_semantics=("arbitrary",),
            vmem_limit_bytes=VMEM_LIMIT_BYTES),
        name="encoder_layer",
    )(x, x, x, w_in, w_out, conv_w, conv_b, sg_w, sg_bias, sg_g, sg_b, ln_g, ln_b)


def _trunk(x, params):
    for l in range(DEPTH):
        x = _layer(x, *(p[l] for p in params))
    return x


def kernel(x_prompt, x_sample, w_in, conv_w, conv_b, sg_w, sg_b, sg_norm_g, sg_norm_b,
           w_out, ln_g, ln_b):
    sg_bias = jnp.repeat(jnp.swapaxes(sg_b, 1, 2), B_HEAD_DIM, axis=2)
    w_out_scaled = jnp.pad((w_out * (1.0 / ALPHA)).astype(BF16),
                           ((0, 0), (0, 0), (0, W_OUT_SPARE_COLS)))
    params = (
        w_in.astype(BF16),
        w_out_scaled,
        conv_w,
        conv_b.reshape(DEPTH, 1, A_WIDTH),
        sg_w.astype(BF16),
        sg_bias,
        sg_norm_g.reshape(DEPTH, 1, B_WIDTH),
        sg_norm_b.reshape(DEPTH, 1, B_WIDTH),
        ln_g.reshape(DEPTH, 1, D_MODEL),
        ln_b.reshape(DEPTH, 1, D_MODEL),
    )
    return (_trunk(x_prompt, params), _trunk(x_sample, params))
```

```python
import functools
import math

import jax
import jax.numpy as jnp
from jax import lax
from jax.experimental import pallas as pl
from jax.experimental.pallas import tpu as pltpu

D_MODEL = 1024
DEPTH = 4
A_WIDTH = 512
B_WIDTH = 512
B_HEADS = 4
B_HEAD_DIM = 128
CHUNK = 128
PROJ_OUT = 4 * A_WIDTH + 3 * B_WIDTH
ALPHA = (2.0 * DEPTH) ** 0.25
LN_EPS = 1e-5
GELU_C0 = math.sqrt(2.0 / math.pi)
GELU_C1 = 0.044715
LOG2_E = math.log2(math.e)

COL_A_IN = 0
COL_A_B = A_WIDTH
COL_A_C = 2 * A_WIDTH
COL_A_Z = 3 * A_WIDTH
COL_B_U = 4 * A_WIDTH
COL_B_V = 4 * A_WIDTH + B_WIDTH
COL_B_Z = 4 * A_WIDTH + 2 * B_WIDTH
W_OUT_SPARE_COLS = 128

HALO = 16
SEQ_TILE = 1024
ROW_BLOCK = 256
N_BLOCKS = SEQ_TILE // ROW_BLOCK
VMEM_LIMIT_BYTES = 56 * 1024 * 1024
F32 = jnp.float32
BF16 = jnp.bfloat16


def _gelu_tanh(x):
    k0 = -2.0 * GELU_C0 * LOG2_E
    return x * (1.0 / (1.0 + jnp.exp2(x * (k0 + (k0 * GELU_C1) * (x * x)))))


def _silu(x):
    return x * (1.0 / (1.0 + jnp.exp2(-LOG2_E * x)))


def _zero_after(x):
    u = pltpu.bitcast(x, jnp.int32)
    acc = u[:, 0:128]
    for c in range(1, x.shape[1] // 128):
        acc = acc | u[:, c * 128:(c + 1) * 128]
    out = acc[0:8]
    for g in range(1, x.shape[0] // 8):
        out = out | acc[g * 8:(g + 1) * 8]
    return lax.shift_right_logical(lax.shift_right_logical(out, 16), 16)


def _ordered_after(w, x):
    if x is None:
        return w
    zero = pltpu.bitcast(_zero_after(x), F32)
    corner = w[0:HALO, 0:128].astype(F32) + jnp.concatenate([zero] * (HALO // 8), axis=0)
    top = jnp.concatenate([corner.astype(BF16), w[0:HALO, 128:]], axis=1)
    return jnp.concatenate([top, w[HALO:]], axis=0)


def _norm_block(r0, r_ref, ln_g_ref, ln_b_ref, o_ref):
    r = r_ref[r0:r0 + ROW_BLOCK, :]
    mu = jnp.mean(r, axis=-1, keepdims=True)
    d = r - mu
    var = jnp.mean(d * d, axis=-1, keepdims=True)
    out = d * lax.rsqrt(var + LN_EPS / (ALPHA * ALPHA)) * ln_g_ref[...] + ln_b_ref[...]
    o_ref[0, r0:r0 + ROW_BLOCK, :] = out
    return out


def _front_block(j, xb_ref, w_ref, sg_w_ref, sg_g_ref, sg_b_ref):
    r0 = j * ROW_BLOCK
    lo = 0 if j == 0 else HALO + r0
    hi = SEQ_TILE + 2 * HALO if j == N_BLOCKS - 1 else HALO + r0 + ROW_BLOCK
    xm = xb_ref[r0 + HALO:r0 + HALO + ROW_BLOCK, :]
    xa = xb_ref[lo:hi, :]

    b_v = jnp.dot(xm, w_ref[:, COL_B_V:COL_B_V + B_WIDTH], preferred_element_type=F32)
    v = _gelu_tanh(b_v)
    n_chunks = ROW_BLOCK // CHUNK
    rhs_heads = []
    for h in range(B_HEADS):
        vh = v[:, h * B_HEAD_DIM:(h + 1) * B_HEAD_DIM]
        mu = jnp.mean(vh, axis=-1, keepdims=True)
        d = vh - mu
        var = jnp.mean(d * d, axis=-1, keepdims=True)
        vn = (d * lax.rsqrt(var + LN_EPS)
              * sg_g_ref[:, h * B_HEAD_DIM:(h + 1) * B_HEAD_DIM]
              + sg_b_ref[:, h * B_HEAD_DIM:(h + 1) * B_HEAD_DIM]).astype(BF16)
        rhs_heads.append(jnp.concatenate(
            [vn[c * CHUNK:(c + 1) * CHUNK] for c in range(n_chunks)], axis=1))

    a_in = jnp.dot(xa, w_ref[:, COL_A_IN:COL_A_IN + A_WIDTH], preferred_element_type=F32)
    a_c = jnp.dot(xa, w_ref[:, COL_A_C:COL_A_C + A_WIDTH], preferred_element_type=F32)

    s_heads = []
    for h in range(B_HEADS):
        s2 = jnp.dot(sg_w_ref[h], rhs_heads[h], preferred_element_type=F32)
        s_heads.append(jnp.concatenate(
            [s2[:, c * B_HEAD_DIM:(c + 1) * B_HEAD_DIM] for c in range(n_chunks)],
            axis=0))
    return jnp.concatenate(s_heads, axis=1), (lo, a_c * a_in)


def _t_rows(pieces, a, b):
    for lo, t in pieces:
        if lo <= a and b <= lo + t.shape[0]:
            return t[a - lo:b - lo]
    raise ValueError("rows span two pieces")


def _back_block(j, s, pieces, x_ref, xb_ref, w_ref, w_out_ref, conv_w_ref, conv_b_ref,
                sg_bias_ref, r_ref, normed=None):
    r0 = j * ROW_BLOCK
    xm = xb_ref[r0 + HALO:r0 + HALO + ROW_BLOCK, :]
    n_chunks = ROW_BLOCK // CHUNK

    first = HALO + r0
    t_0 = _t_rows(pieces, first, first + ROW_BLOCK)
    t_ext = jnp.concatenate([_t_rows(pieces, first - 8, first), t_0,
                             _t_rows(pieces, first + ROW_BLOCK, first + ROW_BLOCK + 8)], axis=0)
    t_m1 = pltpu.roll(t_ext, 1, axis=0)[8:8 + ROW_BLOCK]
    t_p1 = pltpu.roll(t_ext, ROW_BLOCK + 15, axis=0)[8:8 + ROW_BLOCK]
    conv = (conv_w_ref[0:1, :] * t_m1 + conv_w_ref[1:2, :] * t_0
            + conv_w_ref[2:3, :] * t_p1 + conv_b_ref[...])
    quarters = [None] * 4 if normed is None else [
        normed[q * (ROW_BLOCK // 4):(q + 1) * (ROW_BLOCK // 4)] for q in range(4)]
    a_b = jnp.dot(xm, _ordered_after(w_ref[:, COL_A_B:COL_A_B + A_WIDTH], quarters[0]),
                  preferred_element_type=F32)
    a_z = jnp.dot(xm, _ordered_after(w_ref[:, COL_A_Z:COL_A_Z + A_WIDTH], quarters[1]),
                  preferred_element_type=F32)
    out_a = (a_b * conv * _silu(a_z)).astype(BF16)

    s = s + jnp.concatenate([sg_bias_ref[...]] * n_chunks, axis=0)
    b_u = jnp.dot(xm, _ordered_after(w_ref[:, COL_B_U:COL_B_U + B_WIDTH], quarters[2]),
                  preferred_element_type=F32)
    b_z = jnp.dot(xm, _ordered_after(w_ref[:, COL_B_Z:COL_B_Z + B_WIDTH], quarters[3]),
                  preferred_element_type=F32)
    out_b = (_gelu_tanh(b_u) * s * _silu(b_z)).astype(BF16)

    mixed = jnp.concatenate([out_a, out_b], axis=1)
    y = jnp.dot(mixed, w_out_ref[:, 0:D_MODEL], preferred_element_type=F32)
    r_ref[r0:r0 + ROW_BLOCK, :] = x_ref[0, r0:r0 + ROW_BLOCK, :] + y


def _layer_kernel(n_tiles, x_ref, xprev_ref, xnext_ref, w_ref, w_out_ref, conv_w_ref, conv_b_ref,
                  sg_w_ref, sg_bias_ref, sg_g_ref, sg_b_ref, ln_g_ref, ln_b_ref,
                  o_ref, xb_ref, r_ref):
    s = pl.program_id(0)
    n_slabs = pl.num_programs(0) - 1
    i = lax.rem(jnp.minimum(s, n_slabs - 1), n_tiles)

    def stage_in():
        prev = xprev_ref[0].astype(BF16)
        nxt = xnext_ref[0].astype(BF16)
        xb_ref[0:HALO, :] = jnp.where(i > 0, prev, jnp.zeros_like(prev))
        xb_ref[HALO + SEQ_TILE:, :] = jnp.where(i < n_tiles - 1, nxt, jnp.zeros_like(nxt))
        for r0 in range(0, SEQ_TILE, ROW_BLOCK):
            xb_ref[HALO + r0:HALO + r0 + ROW_BLOCK, :] = (
                x_ref[0, r0:r0 + ROW_BLOCK, :].astype(BF16))

    def norm(j):
        return _norm_block(j * ROW_BLOCK, r_ref, ln_g_ref, ln_b_ref, o_ref)

    def slab_rows(with_norm):
        def front(j):
            return _front_block(j, xb_ref, w_ref, sg_w_ref, sg_g_ref, sg_b_ref)

        fronts = {0: front(0)}
        for j in range(N_BLOCKS):
            normed = norm(j) if with_norm else None
            if j + 1 < N_BLOCKS:
                fronts[j + 1] = front(j + 1)
            pieces = [fronts[k][1] for k in (j - 1, j, j + 1) if k in fronts]
            _back_block(j, fronts[j][0], pieces, x_ref, xb_ref, w_ref, w_out_ref, conv_w_ref, conv_b_ref,
                        sg_bias_ref, r_ref, normed)
            fronts.pop(j - 1, None)

    @pl.when(s == 0)
    def _():
        stage_in()
        slab_rows(False)

    @pl.when(jnp.logical_and(s > 0, s < n_slabs))
    def _():
        stage_in()
        slab_rows(True)

    @pl.when(s == n_slabs)
    def _():
        for j in range(N_BLOCKS):
            norm(j)


def _layer(x, layer, w_in, w_out, conv_w, conv_b, sg_w, sg_bias, sg_g, sg_b, ln_g, ln_b):
    bsz, seq, _ = x.shape
    assert seq % SEQ_TILE == 0 and SEQ_TILE % ROW_BLOCK == 0 and ROW_BLOCK % CHUNK == 0
    n_tiles = seq // SEQ_TILE
    n_slabs = bsz * n_tiles
    halo_per_tile = SEQ_TILE // HALO
    n_halo_blocks = seq // HALO

    def slab(s):
        t = jnp.clip(s, 0, n_slabs - 1)
        return t // n_tiles, t % n_tiles

    def x_map(s):
        b, i = slab(s)
        return b, i, 0

    def prev_map(s):
        b, i = slab(s)
        return b, jnp.maximum(i * halo_per_tile - 1, 0), 0

    def next_map(s):
        b, i = slab(s)
        return b, jnp.minimum((i + 1) * halo_per_tile, n_halo_blocks - 1), 0

    def out_map(s):
        b, i = slab(s - 1)
        return b, i, 0

    def const(shape):
        return pl.BlockSpec((None,) + shape, lambda s: (layer,) + (0,) * len(shape))

    in_specs = [
        pl.BlockSpec((1, SEQ_TILE, D_MODEL), x_map),
        pl.BlockSpec((1, HALO, D_MODEL), prev_map),
        pl.BlockSpec((1, HALO, D_MODEL), next_map),
        const((D_MODEL, PROJ_OUT)),
        const((D_MODEL, D_MODEL + W_OUT_SPARE_COLS)),
        const((3, A_WIDTH)),
        const((1, A_WIDTH)),
        const((B_HEADS, CHUNK, CHUNK)),
        const((CHUNK, B_WIDTH)),
        const((1, B_WIDTH)),
        const((1, B_WIDTH)),
        const((1, D_MODEL)),
        const((1, D_MODEL)),
    ]
    return pl.pallas_call(
        functools.partial(_layer_kernel, n_tiles),
        grid=(n_slabs + 1,),
        in_specs=in_specs,
        out_specs=pl.BlockSpec((1, SEQ_TILE, D_MODEL), out_map),
        out_shape=jax.ShapeDtypeStruct(x.shape, jnp.float32),
        scratch_shapes=[pltpu.VMEM((SEQ_TILE + 2 * HALO, D_MODEL), BF16),
                        pltpu.VMEM((SEQ_TILE, D_MODEL), F32)],
        compiler_params=pltpu.CompilerParams(
            dimension_semantics=("arbitrary",),
            vmem_limit_bytes=VMEM_LIMIT_BYTES),
        name="encoder_layer",
    )(x, x, x, w_in, w_out, conv_w, conv_b, sg_w, sg_bias, sg_g, sg_b, ln_g, ln_b)


def _trunk(x, params):
    for l in range(DEPTH):
        x = _layer(x, l, *params)
    return x


def kernel(x_prompt, x_sample, w_in, conv_w, conv_b, sg_w, sg_b, sg_norm_g, sg_norm_b,
           w_out, ln_g, ln_b):
    sg_bias = jnp.repeat(jnp.swapaxes(sg_b, 1, 2), B_HEAD_DIM, axis=2)
    w_out_scaled = jnp.pad((w_out * (1.0 / ALPHA)).astype(BF16),
                           ((0, 0), (0, 0), (0, W_OUT_SPARE_COLS)))
    params = (
        w_in.astype(BF16),
        w_out_scaled,
        conv_w,
        conv_b.reshape(DEPTH, 1, A_WIDTH),
        sg_w.astype(BF16),
        sg_bias,
        sg_norm_g.reshape(DEPTH, 1, B_WIDTH),
        sg_norm_b.reshape(DEPTH, 1, B_WIDTH),
        ln_g.reshape(DEPTH, 1, D_MODEL),
        ln_b.reshape(DEPTH, 1, D_MODEL),
    )
    return (_trunk(x_prompt, params), _trunk(x_sample, params))
```

```python
import functools
import math

import jax
import jax.numpy as jnp
from jax import lax
from jax.experimental import pallas as pl
from jax.experimental.pallas import tpu as pltpu

D_MODEL = 1024
DEPTH = 4
A_WIDTH = 512
B_WIDTH = 512
B_HEADS = 4
B_HEAD_DIM = 128
CHUNK = 128
PROJ_OUT = 4 * A_WIDTH + 3 * B_WIDTH
ALPHA = (2.0 * DEPTH) ** 0.25
LN_EPS = 1e-5
GELU_C0 = math.sqrt(2.0 / math.pi)
GELU_C1 = 0.044715
LOG2_E = math.log2(math.e)

COL_A_IN = 0
COL_A_B = A_WIDTH
COL_A_C = 2 * A_WIDTH
COL_A_Z = 3 * A_WIDTH
COL_B_U = 4 * A_WIDTH
COL_B_V = 4 * A_WIDTH + B_WIDTH
COL_B_Z = 4 * A_WIDTH + 2 * B_WIDTH

HALO = 16
SEQ_TILE = 1024
ROW_BLOCK = 256
N_BLOCKS = SEQ_TILE // ROW_BLOCK
VMEM_LIMIT_BYTES = 56 * 1024 * 1024
F32 = jnp.float32
BF16 = jnp.bfloat16


def _gelu_tanh(x):
    k0 = -2.0 * GELU_C0 * LOG2_E
    return x * (1.0 / (1.0 + jnp.exp2(x * (k0 + (k0 * GELU_C1) * (x * x)))))


def _silu(x):
    return x * (1.0 / (1.0 + jnp.exp2(-LOG2_E * x)))


def _zero_after(x):
    u = pltpu.bitcast(x, jnp.int32)
    acc = u[:, 0:128]
    for c in range(1, x.shape[1] // 128):
        acc = acc | u[:, c * 128:(c + 1) * 128]
    out = acc[0:8]
    for g in range(1, x.shape[0] // 8):
        out = out | acc[g * 8:(g + 1) * 8]
    return lax.shift_right_logical(lax.shift_right_logical(out, 16), 16)


def _ordered_after(w, x):
    if x is None:
        return w
    zero = pltpu.bitcast(_zero_after(x), F32)
    corner = w[0:HALO, 0:128].astype(F32) + jnp.concatenate([zero] * (HALO // 8), axis=0)
    top = jnp.concatenate([corner.astype(BF16), w[0:HALO, 128:]], axis=1)
    return jnp.concatenate([top, w[HALO:]], axis=0)


def _norm_block(r0, r_ref, ln_g_ref, ln_b_ref, o_ref):
    r = r_ref[r0:r0 + ROW_BLOCK, :]
    mu = jnp.mean(r, axis=-1, keepdims=True)
    d = r - mu
    var = jnp.mean(d * d, axis=-1, keepdims=True)
    out = d * lax.rsqrt(var + LN_EPS / (ALPHA * ALPHA)) * ln_g_ref[...] + ln_b_ref[...]
    o_ref[0, r0:r0 + ROW_BLOCK, :] = out
    return out


def _front_block(j, xb_ref, w_ref, sg_w_ref, sg_g_ref, sg_b_ref):
    r0 = j * ROW_BLOCK
    lo = 0 if j == 0 else HALO + r0
    hi = SEQ_TILE + 2 * HALO if j == N_BLOCKS - 1 else HALO + r0 + ROW_BLOCK
    xm = xb_ref[r0 + HALO:r0 + HALO + ROW_BLOCK, :]
    xa = xb_ref[lo:hi, :]

    b_v = jnp.dot(xm, w_ref[:, COL_B_V:COL_B_V + B_WIDTH], preferred_element_type=F32)
    v = _gelu_tanh(b_v)
    n_chunks = ROW_BLOCK // CHUNK
    rhs_heads = []
    for h in range(B_HEADS):
        vh = v[:, h * B_HEAD_DIM:(h + 1) * B_HEAD_DIM]
        mu = jnp.mean(vh, axis=-1, keepdims=True)
        d = vh - mu
        var = jnp.mean(d * d, axis=-1, keepdims=True)
        vn = (d * lax.rsqrt(var + LN_EPS)
              * sg_g_ref[:, h * B_HEAD_DIM:(h + 1) * B_HEAD_DIM]
              + sg_b_ref[:, h * B_HEAD_DIM:(h + 1) * B_HEAD_DIM]).astype(BF16)
        rhs_heads.append(jnp.concatenate(
            [vn[c * CHUNK:(c + 1) * CHUNK] for c in range(n_chunks)], axis=1))

    a_in = jnp.dot(xa, w_ref[:, COL_A_IN:COL_A_IN + A_WIDTH], preferred_element_type=F32)
    a_c = jnp.dot(xa, w_ref[:, COL_A_C:COL_A_C + A_WIDTH], preferred_element_type=F32)

    s_heads = []
    for h in range(B_HEADS):
        s2 = jnp.dot(sg_w_ref[h], rhs_heads[h], preferred_element_type=F32)
        s_heads.append(jnp.concatenate(
            [s2[:, c * B_HEAD_DIM:(c + 1) * B_HEAD_DIM] for c in range(n_chunks)],
            axis=0))
    return jnp.concatenate(s_heads, axis=1), (lo, a_c * a_in)


def _t_rows(pieces, a, b):
    for lo, t in pieces:
        if lo <= a and b <= lo + t.shape[0]:
            return t[a - lo:b - lo]
    raise ValueError("rows span two pieces")


def _back_block(j, s, pieces, x_ref, xb_ref, w_ref, wo_ref, conv_w_ref, conv_b_ref,
                sg_bias_ref, r_ref, normed=None):
    r0 = j * ROW_BLOCK
    xm = xb_ref[r0 + HALO:r0 + HALO + ROW_BLOCK, :]
    n_chunks = ROW_BLOCK // CHUNK

    first = HALO + r0
    t_0 = _t_rows(pieces, first, first + ROW_BLOCK)
    t_ext = jnp.concatenate([_t_rows(pieces, first - 8, first), t_0,
                             _t_rows(pieces, first + ROW_BLOCK, first + ROW_BLOCK + 8)], axis=0)
    t_m1 = pltpu.roll(t_ext, 1, axis=0)[8:8 + ROW_BLOCK]
    t_p1 = pltpu.roll(t_ext, ROW_BLOCK + 15, axis=0)[8:8 + ROW_BLOCK]
    conv = (conv_w_ref[0:1, :] * t_m1 + conv_w_ref[1:2, :] * t_0
            + conv_w_ref[2:3, :] * t_p1 + conv_b_ref[...])
    quarters = [None] * 4 if normed is None else [
        normed[q * (ROW_BLOCK // 4):(q + 1) * (ROW_BLOCK // 4)] for q in range(4)]
    a_b = jnp.dot(xm, _ordered_after(w_ref[:, COL_A_B:COL_A_B + A_WIDTH], quarters[0]),
                  preferred_element_type=F32)
    a_z = jnp.dot(xm, _ordered_after(w_ref[:, COL_A_Z:COL_A_Z + A_WIDTH], quarters[1]),
                  preferred_element_type=F32)
    out_a = (a_b * conv * _silu(a_z)).astype(BF16)

    s = s + jnp.concatenate([sg_bias_ref[...]] * n_chunks, axis=0)
    b_u = jnp.dot(xm, _ordered_after(w_ref[:, COL_B_U:COL_B_U + B_WIDTH], quarters[2]),
                  preferred_element_type=F32)
    b_z = jnp.dot(xm, _ordered_after(w_ref[:, COL_B_Z:COL_B_Z + B_WIDTH], quarters[3]),
                  preferred_element_type=F32)
    out_b = (_gelu_tanh(b_u) * s * _silu(b_z)).astype(BF16)

    mixed = jnp.concatenate([out_a, out_b], axis=1)
    y = jnp.dot(mixed, wo_ref[...], preferred_element_type=F32)
    r_ref[r0:r0 + ROW_BLOCK, :] = x_ref[0, r0:r0 + ROW_BLOCK, :] + y


def _layer_kernel(n_tiles, x_ref, xprev_ref, xnext_ref, w_ref, w_out_ref, conv_w_ref, conv_b_ref,
                  sg_w_ref, sg_bias_ref, sg_g_ref, sg_b_ref, ln_g_ref, ln_b_ref,
                  o_ref, xb_ref, r_ref, wo_ref):
    s = pl.program_id(0)
    n_slabs = pl.num_programs(0) - 1
    i = lax.rem(jnp.minimum(s, n_slabs - 1), n_tiles)

    def stage_in():
        prev = xprev_ref[0].astype(BF16)
        nxt = xnext_ref[0].astype(BF16)
        xb_ref[0:HALO, :] = jnp.where(i > 0, prev, jnp.zeros_like(prev))
        xb_ref[HALO + SEQ_TILE:, :] = jnp.where(i < n_tiles - 1, nxt, jnp.zeros_like(nxt))
        for r0 in range(0, SEQ_TILE, ROW_BLOCK):
            xb_ref[HALO + r0:HALO + r0 + ROW_BLOCK, :] = (
                x_ref[0, r0:r0 + ROW_BLOCK, :].astype(BF16))

    def norm(j):
        return _norm_block(j * ROW_BLOCK, r_ref, ln_g_ref, ln_b_ref, o_ref)

    def slab_rows(with_norm):
        def front(j):
            return _front_block(j, xb_ref, w_ref, sg_w_ref, sg_g_ref, sg_b_ref)

        fronts = {0: front(0)}
        for j in range(N_BLOCKS):
            normed = norm(j) if with_norm else None
            if j + 1 < N_BLOCKS:
                fronts[j + 1] = front(j + 1)
            pieces = [fronts[k][1] for k in (j - 1, j, j + 1) if k in fronts]
            _back_block(j, fronts[j][0], pieces, x_ref, xb_ref, w_ref, wo_ref, conv_w_ref, conv_b_ref,
                        sg_bias_ref, r_ref, normed)
            fronts.pop(j - 1, None)

    @pl.when(s == 0)
    def _():
        wo_ref[...] = (w_out_ref[...] * (1.0 / ALPHA)).astype(BF16)
        stage_in()
        slab_rows(False)

    @pl.when(jnp.logical_and(s > 0, s < n_slabs))
    def _():
        stage_in()
        slab_rows(True)

    @pl.when(s == n_slabs)
    def _():
        for j in range(N_BLOCKS):
            norm(j)


def _layer(x, layer, w_in, w_out, conv_w, conv_b, sg_w, sg_bias, sg_g, sg_b, ln_g, ln_b):
    bsz, seq, _ = x.shape
    assert seq % SEQ_TILE == 0 and SEQ_TILE % ROW_BLOCK == 0 and ROW_BLOCK % CHUNK == 0
    n_tiles = seq // SEQ_TILE
    n_slabs = bsz * n_tiles
    halo_per_tile = SEQ_TILE // HALO
    n_halo_blocks = seq // HALO

    def slab(s):
        t = jnp.clip(s, 0, n_slabs - 1)
        return t // n_tiles, t % n_tiles

    def x_map(s):
        b, i = slab(s)
        return b, i, 0

    def prev_map(s):
        b, i = slab(s)
        return b, jnp.maximum(i * halo_per_tile - 1, 0), 0

    def next_map(s):
        b, i = slab(s)
        return b, jnp.minimum((i + 1) * halo_per_tile, n_halo_blocks - 1), 0

    def out_map(s):
        b, i = slab(s - 1)
        return b, i, 0

    def const(shape):
        return pl.BlockSpec((None,) + shape, lambda s: (layer,) + (0,) * len(shape))

    in_specs = [
        pl.BlockSpec((1, SEQ_TILE, D_MODEL), x_map),
        pl.BlockSpec((1, HALO, D_MODEL), prev_map),
        pl.BlockSpec((1, HALO, D_MODEL), next_map),
        const((D_MODEL, PROJ_OUT)),
        const((D_MODEL, D_MODEL)),
        const((3, A_WIDTH)),
        const((1, A_WIDTH)),
        const((B_HEADS, CHUNK, CHUNK)),
        const((CHUNK, B_WIDTH)),
        const((1, B_WIDTH)),
        const((1, B_WIDTH)),
        const((1, D_MODEL)),
        const((1, D_MODEL)),
    ]
    return pl.pallas_call(
        functools.partial(_layer_kernel, n_tiles),
        grid=(n_slabs + 1,),
        in_specs=in_specs,
        out_specs=pl.BlockSpec((1, SEQ_TILE, D_MODEL), out_map),
        out_shape=jax.ShapeDtypeStruct(x.shape, jnp.float32),
        scratch_shapes=[pltpu.VMEM((SEQ_TILE + 2 * HALO, D_MODEL), BF16),
                        pltpu.VMEM((SEQ_TILE, D_MODEL), F32),
                        pltpu.VMEM((D_MODEL, D_MODEL), BF16)],
        compiler_params=pltpu.CompilerParams(
            dimension_semantics=("arbitrary",),
            vmem_limit_bytes=VMEM_LIMIT_BYTES),
        name="encoder_layer",
    )(x, x, x, w_in, w_out, conv_w, conv_b, sg_w, sg_bias, sg_g, sg_b, ln_g, ln_b)


def _trunk(x, params):
    for l in range(DEPTH):
        x = _layer(x, l, *params)
    return x


def kernel(x_prompt, x_sample, w_in, conv_w, conv_b, sg_w, sg_b, sg_norm_g, sg_norm_b,
           w_out, ln_g, ln_b):
    sg_bias = jnp.repeat(jnp.swapaxes(sg_b, 1, 2), B_HEAD_DIM, axis=2)
    params = (
        w_in.astype(BF16),
        w_out,
        conv_w,
        conv_b.reshape(DEPTH, 1, A_WIDTH),
        sg_w.astype(BF16),
        sg_bias,
        sg_norm_g.reshape(DEPTH, 1, B_WIDTH),
        sg_norm_b.reshape(DEPTH, 1, B_WIDTH),
        ln_g.reshape(DEPTH, 1, D_MODEL),
        ln_b.reshape(DEPTH, 1, D_MODEL),
    )
    return (_trunk(x_prompt, params), _trunk(x_sample, params))
```

```python
import functools
import math

import jax
import jax.numpy as jnp
from jax import lax
from jax.experimental import pallas as pl
from jax.experimental.pallas import tpu as pltpu

D_MODEL = 1024
DEPTH = 4
A_WIDTH = 512
B_WIDTH = 512
B_HEADS = 4
B_HEAD_DIM = 128
CHUNK = 128
PROJ_OUT = 4 * A_WIDTH + 3 * B_WIDTH
ALPHA = (2.0 * DEPTH) ** 0.25
LN_EPS = 1e-5
GELU_C0 = math.sqrt(2.0 / math.pi)
GELU_C1 = 0.044715
LOG2_E = math.log2(math.e)

COL_A_IN = 0
COL_A_B = A_WIDTH
COL_A_C = 2 * A_WIDTH
COL_A_Z = 3 * A_WIDTH
COL_B_U = 4 * A_WIDTH
COL_B_V = 4 * A_WIDTH + B_WIDTH
COL_B_Z = 4 * A_WIDTH + 2 * B_WIDTH

SUBLANES = 8
LANES = 128

W_OUT_SPARE_COLS = LANES

HALO = 2 * SUBLANES
SEQ_TILE = 1024
ROW_BLOCK = 256
N_BLOCKS = SEQ_TILE // ROW_BLOCK
VMEM_LIMIT_BYTES = 56 * 1024 * 1024
F32 = jnp.float32
BF16 = jnp.bfloat16


def _gelu_tanh(x):
    k0 = -2.0 * GELU_C0 * LOG2_E
    return x * (1.0 / (1.0 + jnp.exp2(x * (k0 + (k0 * GELU_C1) * (x * x)))))


def _silu(x):
    return x * (1.0 / (1.0 + jnp.exp2(-LOG2_E * x)))


def _zero_after(x):
    u = pltpu.bitcast(x, jnp.int32)
    acc = u[:, 0:LANES]
    for c in range(1, x.shape[1] // LANES):
        acc = acc | u[:, c * LANES:(c + 1) * LANES]
    out = acc[0:SUBLANES]
    for g in range(1, x.shape[0] // SUBLANES):
        out = out | acc[g * SUBLANES:(g + 1) * SUBLANES]
    return lax.shift_right_logical(lax.shift_right_logical(out, 16), 16)


def _ordered_after(w, x):
    if x is None:
        return w
    zero = pltpu.bitcast(_zero_after(x), F32)
    corner = (w[0:HALO, 0:LANES].astype(F32)
              + jnp.concatenate([zero] * (HALO // SUBLANES), axis=0))
    top = jnp.concatenate([corner.astype(BF16), w[0:HALO, LANES:]], axis=1)
    return jnp.concatenate([top, w[HALO:]], axis=0)


def _norm_block(r0, r_ref, ln_g_ref, ln_b_ref, o_ref):
    r = r_ref[r0:r0 + ROW_BLOCK, :]
    mu = jnp.mean(r, axis=-1, keepdims=True)
    d = r - mu
    var = jnp.mean(d * d, axis=-1, keepdims=True)
    out = d * lax.rsqrt(var + LN_EPS / (ALPHA * ALPHA)) * ln_g_ref[...] + ln_b_ref[...]
    o_ref[0, r0:r0 + ROW_BLOCK, :] = out
    return out


def _front_block(j, xb_ref, w_ref, sg_w_ref, sg_g_ref, sg_b_ref):
    r0 = j * ROW_BLOCK
    lo = 0 if j == 0 else HALO + r0
    hi = SEQ_TILE + 2 * HALO if j == N_BLOCKS - 1 else HALO + r0 + ROW_BLOCK
    xm = xb_ref[r0 + HALO:r0 + HALO + ROW_BLOCK, :]
    xa = xb_ref[lo:hi, :]

    b_v = jnp.dot(xm, w_ref[:, COL_B_V:COL_B_V + B_WIDTH], preferred_element_type=F32)
    v = _gelu_tanh(b_v)
    n_chunks = ROW_BLOCK // CHUNK
    rhs_heads = []
    for h in range(B_HEADS):
        vh = v[:, h * B_HEAD_DIM:(h + 1) * B_HEAD_DIM]
        mu = jnp.mean(vh, axis=-1, keepdims=True)
        d = vh - mu
        var = jnp.mean(d * d, axis=-1, keepdims=True)
        vn = (d * lax.rsqrt(var + LN_EPS)
              * sg_g_ref[:, h * B_HEAD_DIM:(h + 1) * B_HEAD_DIM]
              + sg_b_ref[:, h * B_HEAD_DIM:(h + 1) * B_HEAD_DIM]).astype(BF16)
        rhs_heads.append(jnp.concatenate(
            [vn[c * CHUNK:(c + 1) * CHUNK] for c in range(n_chunks)], axis=1))

    a_in = jnp.dot(xa, w_ref[:, COL_A_IN:COL_A_IN + A_WIDTH], preferred_element_type=F32)
    a_c = jnp.dot(xa, w_ref[:, COL_A_C:COL_A_C + A_WIDTH], preferred_element_type=F32)

    s_heads = []
    for h in range(B_HEADS):
        s2 = jnp.dot(sg_w_ref[h], rhs_heads[h], preferred_element_type=F32)
        s_heads.append(jnp.concatenate(
            [s2[:, c * B_HEAD_DIM:(c + 1) * B_HEAD_DIM] for c in range(n_chunks)],
            axis=0))
    return jnp.concatenate(s_heads, axis=1), (lo, a_c * a_in)


def _t_rows(pieces, a, b):
    for lo, t in pieces:
        if lo <= a and b <= lo + t.shape[0]:
            return t[a - lo:b - lo]
    raise ValueError("rows span two pieces")


def _back_block(j, s, pieces, x_ref, xb_ref, w_ref, w_out_ref, conv_w_ref, conv_b_ref,
                sg_bias_ref, r_ref, normed=None):
    r0 = j * ROW_BLOCK
    xm = xb_ref[r0 + HALO:r0 + HALO + ROW_BLOCK, :]
    n_chunks = ROW_BLOCK // CHUNK

    first = HALO + r0
    t_0 = _t_rows(pieces, first, first + ROW_BLOCK)
    pad = SUBLANES
    t_ext = jnp.concatenate([_t_rows(pieces, first - pad, first), t_0,
                             _t_rows(pieces, first + ROW_BLOCK, first + ROW_BLOCK + pad)], axis=0)
    t_m1 = pltpu.roll(t_ext, 1, axis=0)[pad:pad + ROW_BLOCK]
    t_p1 = pltpu.roll(t_ext, t_ext.shape[0] - 1, axis=0)[pad:pad + ROW_BLOCK]
    conv = (conv_w_ref[0:1, :] * t_m1 + conv_w_ref[1:2, :] * t_0
            + conv_w_ref[2:3, :] * t_p1 + conv_b_ref[...])
    quarters = [None] * 4 if normed is None else [
        normed[q * (ROW_BLOCK // 4):(q + 1) * (ROW_BLOCK // 4)] for q in range(4)]
    a_b = jnp.dot(xm, _ordered_after(w_ref[:, COL_A_B:COL_A_B + A_WIDTH], quarters[0]),
                  preferred_element_type=F32)
    a_z = jnp.dot(xm, _ordered_after(w_ref[:, COL_A_Z:COL_A_Z + A_WIDTH], quarters[1]),
                  preferred_element_type=F32)
    out_a = (a_b * conv * _silu(a_z)).astype(BF16)

    s = s + jnp.concatenate([sg_bias_ref[...]] * n_chunks, axis=0)
    b_u = jnp.dot(xm, _ordered_after(w_ref[:, COL_B_U:COL_B_U + B_WIDTH], quarters[2]),
                  preferred_element_type=F32)
    b_z = jnp.dot(xm, _ordered_after(w_ref[:, COL_B_Z:COL_B_Z + B_WIDTH], quarters[3]),
                  preferred_element_type=F32)
    out_b = (_gelu_tanh(b_u) * s * _silu(b_z)).astype(BF16)

    mixed = jnp.concatenate([out_a, out_b], axis=1)
    y = jnp.dot(mixed, w_out_ref[:, 0:D_MODEL], preferred_element_type=F32)
    r_ref[r0:r0 + ROW_BLOCK, :] = x_ref[0, r0:r0 + ROW_BLOCK, :] + y


def _layer_kernel(n_tiles, x_ref, xprev_ref, xnext_ref, w_ref, w_out_ref, conv_w_ref, conv_b_ref,
                  sg_w_ref, sg_bias_ref, sg_g_ref, sg_b_ref, ln_g_ref, ln_b_ref,
                  o_ref, xb_ref, r_ref):
    s = pl.program_id(0)
    n_slabs = pl.num_programs(0) - 1
    i = lax.rem(jnp.minimum(s, n_slabs - 1), n_tiles)

    def stage_in():
        prev = xprev_ref[0].astype(BF16)
        nxt = xnext_ref[0].astype(BF16)
        xb_ref[0:HALO, :] = jnp.where(i > 0, prev, jnp.zeros_like(prev))
        xb_ref[HALO + SEQ_TILE:, :] = jnp.where(i < n_tiles - 1, nxt, jnp.zeros_like(nxt))
        for r0 in range(0, SEQ_TILE, ROW_BLOCK):
            xb_ref[HALO + r0:HALO + r0 + ROW_BLOCK, :] = (
                x_ref[0, r0:r0 + ROW_BLOCK, :].astype(BF16))

    def norm(j):
        return _norm_block(j * ROW_BLOCK, r_ref, ln_g_ref, ln_b_ref, o_ref)

    def slab_rows(with_norm):
        def front(j):
            return _front_block(j, xb_ref, w_ref, sg_w_ref, sg_g_ref, sg_b_ref)

        fronts = {0: front(0)}
        for j in range(N_BLOCKS):
            normed = norm(j) if with_norm else None
            if j + 1 < N_BLOCKS:
                fronts[j + 1] = front(j + 1)
            pieces = [fronts[k][1] for k in (j - 1, j, j + 1) if k in fronts]
            _back_block(j, fronts[j][0], pieces, x_ref, xb_ref, w_ref, w_out_ref, conv_w_ref, conv_b_ref,
                        sg_bias_ref, r_ref, normed)
            fronts.pop(j - 1, None)

    @pl.when(s == 0)
    def _():
        stage_in()
        slab_rows(False)

    @pl.when(jnp.logical_and(s > 0, s < n_slabs))
    def _():
        stage_in()
        slab_rows(True)

    @pl.when(s == n_slabs)
    def _():
        for j in range(N_BLOCKS):
            norm(j)


def _layer(x, layer, w_in, w_out, conv_w, conv_b, sg_w, sg_bias, sg_g, sg_b, ln_g, ln_b):
    bsz, seq, _ = x.shape
    assert seq % SEQ_TILE == 0 and SEQ_TILE % ROW_BLOCK == 0 and ROW_BLOCK % CHUNK == 0
    n_tiles = seq // SEQ_TILE
    n_slabs = bsz * n_tiles
    halo_per_tile = SEQ_TILE // HALO
    n_halo_blocks = seq // HALO

    def slab(s):
        t = jnp.clip(s, 0, n_slabs - 1)
        return t // n_tiles, t % n_tiles

    def x_map(s):
        b, i = slab(s)
        return b, i, 0

    def prev_map(s):
        b, i = slab(s)
        return b, jnp.maximum(i * halo_per_tile - 1, 0), 0

    def next_map(s):
        b, i = slab(s)
        return b, jnp.minimum((i + 1) * halo_per_tile, n_halo_blocks - 1), 0

    def out_map(s):
        b, i = slab(s - 1)
        return b, i, 0

    def const(shape):
        return pl.BlockSpec((None,) + shape, lambda s: (layer,) + (0,) * len(shape))

    in_specs = [
        pl.BlockSpec((1, SEQ_TILE, D_MODEL), x_map),
        pl.BlockSpec((1, HALO, D_MODEL), prev_map),
        pl.BlockSpec((1, HALO, D_MODEL), next_map),
        const((D_MODEL, PROJ_OUT)),
        const((D_MODEL, D_MODEL + W_OUT_SPARE_COLS)),
        const((3, A_WIDTH)),
        const((1, A_WIDTH)),
        const((B_HEADS, CHUNK, CHUNK)),
        const((CHUNK, B_WIDTH)),
        const((1, B_WIDTH)),
        const((1, B_WIDTH)),
        const((1, D_MODEL)),
        const((1, D_MODEL)),
    ]
    return pl.pallas_call(
        functools.partial(_layer_kernel, n_tiles),
        grid=(n_slabs + 1,),
        in_specs=in_specs,
        out_specs=pl.BlockSpec((1, SEQ_TILE, D_MODEL), out_map),
        out_shape=jax.ShapeDtypeStruct(x.shape, jnp.float32),
        scratch_shapes=[pltpu.VMEM((SEQ_TILE + 2 * HALO, D_MODEL), BF16),
                        pltpu.VMEM((SEQ_TILE, D_MODEL), F32)],
        compiler_params=pltpu.CompilerParams(
            dimension_semantics=("arbitrary",),
            vmem_limit_bytes=VMEM_LIMIT_BYTES),
        name="encoder_layer",
    )(x, x, x, w_in, w_out, conv_w, conv_b, sg_w, sg_bias, sg_g, sg_b, ln_g, ln_b)


def _trunk(x, params):
    for l in range(DEPTH):
        x = _layer(x, l, *params)
    return x


def kernel(x_prompt, x_sample, w_in, conv_w, conv_b, sg_w, sg_b, sg_norm_g, sg_norm_b,
           w_out, ln_g, ln_b):
    sg_bias = jnp.repeat(jnp.swapaxes(sg_b, 1, 2), B_HEAD_DIM, axis=2)
    w_out_scaled = jnp.pad((w_out * (1.0 / ALPHA)).astype(BF16),
                           ((0, 0), (0, 0), (0, W_OUT_SPARE_COLS)))
    params = (
        w_in.astype(BF16),
        w_out_scaled,
        conv_w,
        conv_b.reshape(DEPTH, 1, A_WIDTH),
        sg_w.astype(BF16),
        sg_bias,
        sg_norm_g.reshape(DEPTH, 1, B_WIDTH),
        sg_norm_b.reshape(DEPTH, 1, B_WIDTH),
        ln_g.reshape(DEPTH, 1, D_MODEL),
        ln_b.reshape(DEPTH, 1, D_MODEL),
    )
    return (_trunk(x_prompt, params), _trunk(x_sample, params))
```
